```python
import numpy as np
import jax, jax.numpy as jnp
from jax import lax

D_MODEL = 1024
BATCH = 2
SEQ = 8192
DEPTH = 1

CHUNK = 64
HEAD_DIM = 64
N_HEADS_SB = 8
N_HEADS_CA = 8
D_SB = N_HEADS_SB * HEAD_DIM
D_CA = N_HEADS_CA * HEAD_DIM
D_MIX = D_SB + D_CA
QUERY_BLOCK = 128
LEFT_CHUNKS = 8
BAND = (LEFT_CHUNKS + 1) * CHUNK
REL_CLIP = 128
N_GROUPS = 4
EXPERTS_PER_GROUP = 8
TOP_K = 2
D_EXPERT = 256
LN_EPS = 1e-5
DEEPNORM_ALPHA = (2.0 * DEPTH) ** 0.25
DEEPNORM_BETA = (8.0 * DEPTH) ** -0.25

kernel_name = "hybrid_stickbreak_chunkrel_hmoe_deepnorm"


def layer_norm(x, g, b):
    xf = x.astype(jnp.float32)
    mu = jnp.mean(xf, axis=-1, keepdims=True)
    var = jnp.mean(jnp.square(xf - mu), axis=-1, keepdims=True)
    return ((xf - mu) * lax.rsqrt(var + LN_EPS)).astype(x.dtype) * g + b


def stick_breaking_attention(q, k, v):
    seq = q.shape[2]
    scale = HEAD_DIM ** -0.5
    outs = []
    for start in range(0, seq, QUERY_BLOCK):
        end = start + QUERY_BLOCK
        qb = q[:, :, start:end]
        kp = k[:, :, :end]
        vp = v[:, :, :end]
        z = jnp.einsum('bhqd,bhkd->bhqk', qb, kp).astype(jnp.float32) * scale
        t_pos = start + jnp.arange(QUERY_BLOCK)[:, None]
        s_pos = jnp.arange(end)[None, :]
        causal = s_pos < t_pos
        log_fail = jnp.where(causal, jax.nn.log_sigmoid(-z), 0.0)
        between = lax.cumsum(log_fail, axis=3, reverse=True) - log_fail
        w = jnp.where(causal, jnp.exp(jax.nn.log_sigmoid(z) + between), 0.0)
        outs.append(jnp.einsum('bhqk,bhkd->bhqd', w.astype(v.dtype), vp))
    return jnp.concatenate(outs, axis=2)


def chunked_rel_attention(q, k, v, rel_bias):
    b, h, s, dh = q.shape
    nc = s // CHUNK
    scale = HEAD_DIM ** -0.5
    qc = q.reshape(b, h, nc, CHUNK, dh)

    def band(t):
        t = t.reshape(b, h, nc, CHUNK, dh)
        tp = jnp.pad(t, ((0, 0), (0, 0), (LEFT_CHUNKS, 0), (0, 0), (0, 0)))
        return jnp.concatenate([tp[:, :, i:i + nc] for i in range(LEFT_CHUNKS + 1)], axis=3)

    kb, vb = band(k), band(v)
    scores = jnp.einsum('bhcqd,bhckd->bhcqk', qc, kb).astype(jnp.float32) * scale
    rel = np.arange(CHUNK)[:, None] + LEFT_CHUNKS * CHUNK - np.arange(BAND)[None, :]
    rel_idx = np.clip(rel, -REL_CLIP, REL_CLIP) + REL_CLIP
    bias = rel_bias[:, rel_idx].astype(jnp.float32)
    chunk_ok = (np.arange(nc)[:, None] - LEFT_CHUNKS + np.arange(LEFT_CHUNKS + 1)[None, :]) >= 0
    key_ok = np.repeat(chunk_ok, CHUNK, axis=1)
    scores = scores + bias[None, :, None]
    scores = jnp.where(key_ok[None, None, :, None, :], scores, -1e30)
    p = jax.nn.softmax(scores, axis=-1)
    out = jnp.einsum('bhcqk,bhckd->bhcqd', p.astype(v.dtype), vb)
    return out.reshape(b, h, s, dh)


def hybrid_mixer(x, w_in, w_out, rel_bias):
    b, s, _ = x.shape
    proj = x @ w_in
    cuts = list(np.cumsum([D_SB, D_SB, D_SB, D_CA, D_CA]))
    qa, ka, va, qc, kc, vc = jnp.split(proj, cuts, axis=-1)

    def heads(t, n):
        return t.reshape(b, s, n, HEAD_DIM).transpose(0, 2, 1, 3)

    def merge(t):
        return t.transpose(0, 2, 1, 3).reshape(b, s, -1)

    ya = stick_breaking_attention(heads(qa, N_HEADS_SB), heads(ka, N_HEADS_SB), heads(va, N_HEADS_SB))
    yc = chunked_rel_attention(heads(qc, N_HEADS_CA), heads(kc, N_HEADS_CA), heads(vc, N_HEADS_CA), rel_bias)
    y = jnp.concatenate([merge(ya), merge(yc)], axis=-1)
    return y @ w_out


def hierarchical_moe(x, w_group, b_group, w_expert, b_expert, w_gate, w_up, w_down):
    g_logits = (x @ w_group).astype(jnp.float32) + b_group
    g_prob = jax.nn.softmax(g_logits, axis=-1)
    g_gate, g_idx = lax.top_k(g_prob, 1)
    e_logits_all = jnp.einsum('bsd,gde->bsge', x, w_expert).astype(jnp.float32) + b_expert
    e_logits = jnp.take_along_axis(e_logits_all, g_idx[..., None], axis=2)[:, :, 0]
    e_prob = jax.nn.softmax(e_logits, axis=-1)
    e_gate, e_idx = lax.top_k(e_prob, TOP_K)
    e_gate = e_gate / jnp.sum(e_gate, axis=-1, keepdims=True)
    combine = jnp.sum(jax.nn.one_hot(e_idx, EXPERTS_PER_GROUP, dtype=jnp.float32)
                      * e_gate[..., None], axis=2) * g_gate
    group_mask = jax.nn.one_hot(g_idx[..., 0], N_GROUPS, dtype=jnp.float32)
    y = jnp.zeros_like(x)
    for g in range(N_GROUPS):
        cw = (combine * group_mask[..., g:g + 1]).astype(x.dtype)
        hid = jax.nn.silu(jnp.einsum('bsd,edf->bsef', x, w_gate[g])) * jnp.einsum('bsd,edf->bsef', x, w_up[g])
        y = y + jnp.einsum('bsef,efd->bsd', hid * cw[..., None], w_down[g])
    return y


def setup_inputs(seed: int = 0) -> dict:
    key = jax.random.key(seed)
    ks = jax.random.split(key, 16)
    f32 = jnp.float32
    d = D_MODEL
    x = jax.random.normal(ks[0], (BATCH, SEQ, d), f32)
    w_in = jax.random.normal(ks[1], (DEPTH, d, 3 * D_SB + 3 * D_CA), f32) * d ** -0.5
    col_scale = jnp.concatenate([
        jnp.ones((2 * D_SB,), f32), jnp.full((D_SB,), DEEPNORM_BETA, f32),
        jnp.ones((2 * D_CA,), f32), jnp.full((D_CA,), DEEPNORM_BETA, f32)])
    w_in = w_in * col_scale
    w_out = jax.random.normal(ks[2], (DEPTH, D_MIX, d), f32) * D_MIX ** -0.5 * DEEPNORM_BETA
    rel_bias = jax.random.normal(ks[3], (DEPTH, N_HEADS_CA, 2 * REL_CLIP + 1), f32) * 0.1
    ln1_g = 1.0 + 0.01 * jax.random.normal(ks[4], (DEPTH, d), f32)
    ln1_b = 0.01 * jax.random.normal(ks[5], (DEPTH, d), f32)
    w_group = jax.random.normal(ks[6], (DEPTH, d, N_GROUPS), f32) * d ** -0.5
    b_group = 0.01 * jax.random.normal(ks[7], (DEPTH, N_GROUPS), f32)
    w_expert = jax.random.normal(ks[8], (DEPTH, N_GROUPS, d, EXPERTS_PER_GROUP), f32) * d ** -0.5
    b_expert = 0.01 * jax.random.normal(ks[9], (DEPTH, N_GROUPS, EXPERTS_PER_GROUP), f32)
    eshape = (DEPTH, N_GROUPS, EXPERTS_PER_GROUP)
    w_gate = jax.random.normal(ks[10], eshape + (d, D_EXPERT), f32) * d ** -0.5
    w_up = jax.random.normal(ks[11], eshape + (d, D_EXPERT), f32) * d ** -0.5 * DEEPNORM_BETA
    w_down = jax.random.normal(ks[12], eshape + (D_EXPERT, d), f32) * D_EXPERT ** -0.5 * DEEPNORM_BETA
    ln2_g = 1.0 + 0.01 * jax.random.normal(ks[13], (DEPTH, d), f32)
    ln2_b = 0.01 * jax.random.normal(ks[14], (DEPTH, d), f32)
    return {"x": x, "w_in": w_in, "w_out": w_out, "rel_bias": rel_bias,
            "ln1_g": ln1_g, "ln1_b": ln1_b, "w_group": w_group, "b_group": b_group,
            "w_expert": w_expert, "b_expert": b_expert, "w_gate": w_gate, "w_up": w_up,
            "w_down": w_down, "ln2_g": ln2_g, "ln2_b": ln2_b}


def reference(x, w_in, w_out, rel_bias, ln1_g, ln1_b, w_group, b_group, w_expert, b_expert,
              w_gate, w_up, w_down, ln2_g, ln2_b):
    for l in range(DEPTH):
        mix = hybrid_mixer(x, w_in[l], w_out[l], rel_bias[l])
        x = layer_norm(DEEPNORM_ALPHA * x + mix, ln1_g[l], ln1_b[l])
        ffn = hierarchical_moe(x, w_group[l], b_group[l], w_expert[l], b_expert[l],
                               w_gate[l], w_up[l], w_down[l])
        x = layer_norm(DEEPNORM_ALPHA * x + ffn, ln2_g[l], ln2_b[l])
    return x
```

```python
import functools

import numpy as np
import jax
import jax.numpy as jnp
from jax import lax
from jax.experimental import pallas as pl
from jax.experimental.pallas import tpu as pltpu

F32 = jnp.float32
BF16 = jnp.bfloat16

HEAD_DIM = 64
N_HEADS = 8
D_GRP = N_HEADS * HEAD_DIM
CHUNK = 64
LEFT_CHUNKS = 8
REL_CLIP = 128
N_GROUPS = 4
EXPERTS_PER_GROUP = 8
N_EXPERTS = N_GROUPS * EXPERTS_PER_GROUP
LN_EPS = 1e-5
NEG_BIG = -1e30

SUBLANES = 8
LANES = 128
BLK = 256
ROWGROUPS = BLK // SUBLANES
VMEM_LIMIT = 56 * 1024 * 1024


def _nt_dot(a, b):
    return lax.dot_general(a, b, (((1,), (1,)), ((), ())), preferred_element_type=F32)


def _tn_dot(a, b):
    return lax.dot_general(a, b, (((0,), (0,)), ((), ())), preferred_element_type=F32)


def _proj_kernel(x_ref, perm_ref, wnat_ref, wva_ref, wka_ref, wkc_ref,
                 qa_ref, qc_ref, vc_ref, va_ref, ka_ref, kc_ref):
    xb = x_ref[0].astype(BF16)
    xp = jnp.dot(perm_ref[...], xb, preferred_element_type=F32).astype(BF16)
    nat = _nt_dot(wnat_ref[...], xb)
    qa_ref[0] = nat[0:D_GRP].astype(BF16)
    qc_ref[0] = nat[D_GRP:2 * D_GRP].astype(BF16)
    vc = nat[2 * D_GRP:3 * D_GRP].astype(BF16)
    va = _nt_dot(wva_ref[...], xp).astype(BF16)
    ka = jnp.dot(xp, wka_ref[...], preferred_element_type=F32).astype(BF16)
    kc = jnp.dot(xb, wkc_ref[...], preferred_element_type=F32).astype(BF16)
    for h in range(N_HEADS):
        sl = slice(h * HEAD_DIM, (h + 1) * HEAD_DIM)
        vc_ref[0, h, 0] = vc[sl]
        va_ref[0, h, 0] = va[sl]
        ka_ref[0, h, 0] = ka[:, sl]
        kc_ref[0, h, 0] = kc[:, sl]


def _key_permutation():
    r = np.arange(BLK)
    src = (r % SUBLANES) * ROWGROUPS + r // SUBLANES
    return jnp.asarray(np.eye(BLK, dtype=np.float32)[src], dtype=BF16)


def _proj_call(x, wnat, wva, wka, wkc):
    b, s, d = x.shape
    nb = s // BLK
    perm = _key_permutation()
    qspec = pl.BlockSpec((1, D_GRP, BLK), lambda bi, t: (bi, 0, t))
    vspec = pl.BlockSpec((1, N_HEADS, 1, HEAD_DIM, BLK), lambda bi, t: (bi, 0, t, 0, 0))
    kspec = pl.BlockSpec((1, N_HEADS, 1, BLK, HEAD_DIM), lambda bi, t: (bi, 0, t, 0, 0))
    qshape = jax.ShapeDtypeStruct((b, D_GRP, s), BF16)
    vshape = jax.ShapeDtypeStruct((b, N_HEADS, nb, HEAD_DIM, BLK), BF16)
    kshape = jax.ShapeDtypeStruct((b, N_HEADS, nb, BLK, HEAD_DIM), BF16)
    full = lambda a: pl.BlockSpec(a.shape, lambda bi, t: (0,) * a.ndim)
    return pl.pallas_call(
        _proj_kernel,
        grid=(b, nb),
        in_specs=[pl.BlockSpec((1, BLK, d), lambda bi, t: (bi, t, 0)),
                  full(perm), full(wnat), full(wva), full(wka), full(wkc)],
        out_specs=[qspec, qspec, vspec, vspec, kspec, kspec],
        out_shape=[qshape, qshape, vshape, vshape, kshape, kshape],
        compiler_params=pltpu.CompilerParams(
            dimension_semantics=("parallel", "parallel"), vmem_limit_bytes=VMEM_LIMIT),
        name="proj",
    )(x, perm, wnat, wva, wka, wkc)


def _sublane_shift_up(t, d, sub):
    rolled = pltpu.roll(t, SUBLANES - d, axis=0)
    return jnp.where(sub < SUBLANES - d, rolled, 0.0)


def _stick_block(z, carry, v_blk, diag_cols):
    n = z.shape[1]
    sub = lax.broadcasted_iota(jnp.int32, (SUBLANES, n), 0)
    run = jnp.zeros((SUBLANES, n), F32)
    u = [None] * ROWGROUPS
    masks = [None] * ROWGROUPS
    for i in reversed(range(ROWGROUPS)):
        zi = z[i * SUBLANES:(i + 1) * SUBLANES, :]
        sp = jnp.maximum(zi, 0.0) + jnp.log(1.0 + jnp.exp(-jnp.abs(zi)))
        if diag_cols is not None:
            masks[i] = diag_cols > i
            sp = jnp.where(masks[i], sp, 0.0)
        run = run + sp
        u[i] = zi - run
    s1 = run + _sublane_shift_up(run, 1, sub)
    s2 = s1 + _sublane_shift_up(s1, 2, sub)
    s3 = s2 + _sublane_shift_up(s2, 4, sub)
    offset = (s3 - run) + carry
    new_carry = jnp.broadcast_to(s3[0:1, :], (SUBLANES, n)) + carry
    w_rows = []
    for i in range(ROWGROUPS):
        w = jnp.exp(u[i] - offset)
        if diag_cols is not None:
            w = jnp.where(masks[i], w, 0.0)
        w_rows.append(w)
    w_t = jnp.concatenate(w_rows, axis=0).astype(BF16)
    return jnp.dot(v_blk, w_t, preferred_element_type=F32), new_carry


def _stick_kernel(q_ref, k_ref, v_ref, o_ref):
    qi = pl.program_id(2)
    q_t = q_ref[0]
    sub = lax.broadcasted_iota(jnp.int32, (SUBLANES, BLK), 0)
    col = lax.broadcasted_iota(jnp.int32, (SUBLANES, BLK), 1)
    diag_cols = col - ROWGROUPS * sub

    z = jnp.dot(k_ref[0, 0, qi], q_t, preferred_element_type=F32)
    acc, carry = _stick_block(z, jnp.zeros((SUBLANES, BLK), F32), v_ref[0, 0, qi], diag_cols)

    def body(t, state):
        acc, carry = state
        kb = qi - 1 - t
        z = jnp.dot(k_ref[0, 0, kb], q_t, preferred_element_type=F32)
        out, carry = _stick_block(z, carry, v_ref[0, 0, kb], None)
        return acc + out, carry

    acc, _ = lax.fori_loop(0, qi, body, (acc, carry))
    o_ref[0] = acc.astype(BF16)


def _stick_call(q_t, k, v_t):
    b, _, s = q_t.shape
    nb = s // BLK
    return pl.pallas_call(
        _stick_kernel,
        grid=(b, N_HEADS, nb),
        in_specs=[pl.BlockSpec((1, HEAD_DIM, BLK), lambda bi, h, qi: (bi, h, qi)),
                  pl.BlockSpec((1, 1, nb, BLK, HEAD_DIM), lambda bi, h, qi: (bi, h, 0, 0, 0)),
                  pl.BlockSpec((1, 1, nb, HEAD_DIM, BLK), lambda bi, h, qi: (bi, h, 0, 0, 0))],
        out_specs=pl.BlockSpec((1, HEAD_DIM, BLK), lambda bi, h, qi: (bi, h, qi)),
        out_shape=jax.ShapeDtypeStruct((b, D_GRP, s), BF16),
        compiler_params=pltpu.CompilerParams(
            dimension_semantics=("parallel", "parallel", "arbitrary"), vmem_limit_bytes=VMEM_LIMIT),
        name="stick",
    )(q_t, k, v_t)


WIN_BLOCKS = LEFT_CHUNKS * CHUNK // BLK + 1
TABLE_ROWS = (2 * WIN_BLOCKS - 1) * BLK


def _chunk_kernel(q_ref, k_ref, v_ref, bm_ref, o_ref):
    qi = pl.program_id(2)
    kb0 = jnp.maximum(qi - (WIN_BLOCKS - 1), 0)
    row0 = (WIN_BLOCKS - 1 - (qi - kb0)) * BLK
    q_t = q_ref[0]
    scores = []
    for j in range(WIN_BLOCKS):
        start = pl.multiple_of(row0 + j * BLK, BLK)
        sc = jnp.dot(k_ref[0, 0, kb0 + j], q_t, preferred_element_type=F32)
        scores.append(sc + bm_ref[0, pl.ds(start, BLK), :])
    m = jnp.max(scores[0], axis=0, keepdims=True)
    for sc in scores[1:]:
        m = jnp.maximum(m, jnp.max(sc, axis=0, keepdims=True))
    denom = jnp.zeros((1, BLK), F32)
    out = jnp.zeros((HEAD_DIM, BLK), F32)
    for j in range(WIN_BLOCKS):
        p = jnp.exp(scores[j] - m)
        denom = denom + jnp.sum(p, axis=0, keepdims=True)
        out = out + jnp.dot(v_ref[0, 0, kb0 + j], p.astype(BF16), preferred_element_type=F32)
    o_ref[0] = (out / denom).astype(BF16)


def _chunk_call(q_t, k, v_t, bias_mask):
    b, _, s = q_t.shape
    nb = s // BLK
    return pl.pallas_call(
        _chunk_kernel,
        grid=(b, N_HEADS, nb),
        in_specs=[pl.BlockSpec((1, HEAD_DIM, BLK), lambda bi, h, qi: (bi, h, qi)),
                  pl.BlockSpec((1, 1, nb, BLK, HEAD_DIM), lambda bi, h, qi: (bi, h, 0, 0, 0)),
                  pl.BlockSpec((1, 1, nb, HEAD_DIM, BLK), lambda bi, h, qi: (bi, h, 0, 0, 0)),
                  pl.BlockSpec((1, TABLE_ROWS, BLK), lambda bi, h, qi: (h, 0, 0))],
        out_specs=pl.BlockSpec((1, HEAD_DIM, BLK), lambda bi, h, qi: (bi, h, qi)),
        out_shape=jax.ShapeDtypeStruct((b, D_GRP, s), BF16),
        compiler_params=pltpu.CompilerParams(
            dimension_semantics=("parallel", "parallel", "arbitrary"), vmem_limit_bytes=VMEM_LIMIT),
        name="chunk",
    )(q_t, k, v_t, bias_mask)


def _bias_mask_table(rel_bias):
    key = np.arange(TABLE_ROWS)[:, None] - (WIN_BLOCKS - 1) * BLK
    qry = np.arange(BLK)[None, :]
    rel_idx = np.clip(qry - key, -REL_CLIP, REL_CLIP) + REL_CLIP
    dchunk = qry // CHUNK - np.floor_divide(key, CHUNK)
    visible = (dchunk >= 0) & (dchunk <= LEFT_CHUNKS)
    return jnp.where(visible[None], rel_bias[:, rel_idx].astype(F32), NEG_BIG)


def _layer_norm(v, g, b):
    mu = jnp.mean(v, axis=-1, keepdims=True)
    c = v - mu
    var = jnp.mean(c * c, axis=-1, keepdims=True)
    return c * lax.rsqrt(var + LN_EPS) * g + b


def _post_kernel(alpha, ya_ref, yc_ref, x_ref, woa_ref, woc_ref, g_ref, b_ref, wr_ref, br_ref,
                 h_ref, hb_ref, cw_ref):
    mix = _tn_dot(ya_ref[0], woa_ref[...]) + _tn_dot(yc_ref[0], woc_ref[...])
    h = _layer_norm(alpha * x_ref[0] + mix, g_ref[...], b_ref[...])
    h_ref[...] = h
    hb_ref[...] = h.astype(BF16)

    logits = jnp.dot(h, wr_ref[...], preferred_element_type=F32,
                     precision=lax.Precision.HIGHEST) + br_ref[...]
    lane = lax.broadcasted_iota(jnp.int32, logits.shape, 1)
    big = jnp.int32(LANES)
    is_g = (lane >= N_EXPERTS) & (lane < N_EXPERTS + N_GROUPS)
    gl = jnp.where(is_g, logits, NEG_BIG)
    gmax = jnp.max(gl, axis=-1, keepdims=True)
    g_gate = 1.0 / jnp.sum(jnp.exp(gl - gmax), axis=-1, keepdims=True)
    g_idx = jnp.min(jnp.where(gl == gmax, lane, big), axis=-1, keepdims=True) - N_EXPERTS
    in_grp = (lane >= g_idx * EXPERTS_PER_GROUP) & (lane < (g_idx + 1) * EXPERTS_PER_GROUP)
    el = jnp.where(in_grp, logits, NEG_BIG)
    e1 = jnp.max(el, axis=-1, keepdims=True)
    i1 = jnp.min(jnp.where(el == e1, lane, big), axis=-1, keepdims=True)
    el2 = jnp.where(lane == i1, NEG_BIG, el)
    e2 = jnp.max(el2, axis=-1, keepdims=True)
    i2 = jnp.min(jnp.where(el2 == e2, lane, big), axis=-1, keepdims=True)
    r = jnp.exp(e2 - e1)
    gate1 = g_gate / (1.0 + r)
    gate2 = gate1 * r
    cw_ref[...] = jnp.where(lane == i1, gate1, jnp.where(lane == i2, gate2, 0.0))


def _post_call(ya_t, yc_t, x, woa, woc, g, b, wr, br, alpha):
    bsz, s, d = x.shape
    nb = s // BLK
    n = bsz * s
    row = lambda bi, t: (bi * nb + t, 0)
    full = lambda a: pl.BlockSpec(a.shape, lambda bi, t: (0,) * a.ndim)
    return pl.pallas_call(
        functools.partial(_post_kernel, alpha),
        grid=(bsz, nb),
        in_specs=[pl.BlockSpec((1, D_GRP, BLK), lambda bi, t: (bi, 0, t)),
                  pl.BlockSpec((1, D_GRP, BLK), lambda bi, t: (bi, 0, t)),
                  pl.BlockSpec((1, BLK, d), lambda bi, t: (bi, t, 0)),
                  full(woa), full(woc), full(g), full(b), full(wr), full(br)],
        out_specs=[pl.BlockSpec((BLK, d), row), pl.BlockSpec((BLK, d), row),
                   pl.BlockSpec((BLK, LANES), row)],
        out_shape=[jax.ShapeDtypeStruct((n, d), F32), jax.ShapeDtypeStruct((n, d), BF16),
                   jax.ShapeDtypeStruct((n, LANES), F32)],
        compiler_params=pltpu.CompilerParams(
            dimension_semantics=("parallel", "parallel"), vmem_limit_bytes=VMEM_LIMIT),
        name="post",
    )(ya_t, yc_t, x, woa, woc, g, b, wr, br)


MOE_TM = 512


def _moe_kernel(alpha, hb_ref, h_ref, cw_ref, wg_ref, wu_ref, wd_ref, g_ref, b_ref, o_ref, acc_ref):
    e = pl.program_id(1)

    @pl.when(e == 0)
    def _():
        acc_ref[...] = jnp.zeros_like(acc_ref)

    hb = hb_ref[...]
    gate = jnp.dot(hb, wg_ref[0], preferred_element_type=F32)
    up = jnp.dot(hb, wu_ref[0], preferred_element_type=F32)
    lane = lax.broadcasted_iota(jnp.int32, cw_ref.shape, 1)
    cw = jnp.sum(jnp.where(lane == e, cw_ref[...], 0.0), axis=-1, keepdims=True)
    hid = gate * (1.0 / (1.0 + jnp.exp(-gate))) * up * cw
    acc_ref[...] += jnp.dot(hid.astype(BF16), wd_ref[0], preferred_element_type=F32)

    @pl.when(e == pl.num_programs(1) - 1)
    def _():
        o_ref[...] = _layer_norm(alpha * h_ref[...] + acc_ref[...], g_ref[...], b_ref[...])


def _moe_call(hb, h, cw, wg, wu, wd, g, b, alpha):
    n, d = h.shape
    ne, _, df = wg.shape
    tok = lambda t, e: (t, 0)
    full = lambda a: pl.BlockSpec(a.shape, lambda t, e: (0,) * a.ndim)
    return pl.pallas_call(
        functools.partial(_moe_kernel, alpha),
        grid=(n // MOE_TM, ne),
        in_specs=[pl.BlockSpec((MOE_TM, d), tok), pl.BlockSpec((MOE_TM, d), tok),
                  pl.BlockSpec((MOE_TM, LANES), tok),
                  pl.BlockSpec((1, d, df), lambda t, e: (e, 0, 0)),
                  pl.BlockSpec((1, d, df), lambda t, e: (e, 0, 0)),
                  pl.BlockSpec((1, df, d), lambda t, e: (e, 0, 0)),
                  full(g), full(b)],
        out_specs=pl.BlockSpec((MOE_TM, d), tok),
        out_shape=jax.ShapeDtypeStruct((n, d), F32),
        scratch_shapes=[pltpu.VMEM((MOE_TM, d), F32)],
        compiler_params=pltpu.CompilerParams(
            dimension_semantics=("parallel", "arbitrary"), vmem_limit_bytes=VMEM_LIMIT),
        name="moe",
    )(hb, h, cw, wg, wu, wd, g, b)


def _layer(x, w_in, w_out, rel_bias, ln1_g, ln1_b, w_group, b_group, w_expert, b_expert,
           w_gate, w_up, w_down, ln2_g, ln2_b, alpha):
    bsz, s, d = x.shape
    scale = HEAD_DIM ** -0.5
    wqa, wka, wva, wqc, wkc, wvc = [w_in[:, i * D_GRP:(i + 1) * D_GRP] for i in range(6)]
    wnat = jnp.concatenate([wqa * scale, wqc * scale, wvc], axis=1).T.astype(BF16)
    qa_t, qc_t, vc_t, va_t, ka, kc = _proj_call(
        x, wnat, wva.T.astype(BF16), wka.astype(BF16), wkc.astype(BF16))

    ya_t = _stick_call(qa_t, ka, va_t)
    yc_t = _chunk_call(qc_t, kc, vc_t, _bias_mask_table(rel_bias))

    wo = w_out.astype(BF16)
    wr = jnp.zeros((d, LANES), F32)
    wr = wr.at[:, :N_EXPERTS].set(w_expert.transpose(1, 0, 2).reshape(d, N_EXPERTS))
    wr = wr.at[:, N_EXPERTS:N_EXPERTS + N_GROUPS].set(w_group)
    br = jnp.zeros((1, LANES), F32)
    br = br.at[0, :N_EXPERTS].set(b_expert.reshape(N_EXPERTS))
    br = br.at[0, N_EXPERTS:N_EXPERTS + N_GROUPS].set(b_group)
    h, hb, cw = _post_call(ya_t, yc_t, x, wo[:D_GRP], wo[D_GRP:], ln1_g[None], ln1_b[None],
                           wr, br, alpha)

    df = w_gate.shape[-1]
    out = _moe_call(hb, h, cw,
                    w_gate.reshape(N_EXPERTS, d, df).astype(BF16),
                    w_up.reshape(N_EXPERTS, d, df).astype(BF16),
                    w_down.reshape(N_EXPERTS, df, d).astype(BF16),
                    ln2_g[None], ln2_b[None], alpha)
    return out.reshape(bsz, s, d)


def kernel(x, w_in, w_out, rel_bias, ln1_g, ln1_b, w_group, b_group, w_expert, b_expert,
           w_gate, w_up, w_down, ln2_g, ln2_b):
    depth = w_in.shape[0]
    alpha = (2.0 * depth) ** 0.25
    for l in range(depth):
        x = _layer(x, w_in[l], w_out[l], rel_bias[l], ln1_g[l], ln1_b[l], w_group[l], b_group[l],
                   w_expert[l], b_expert[l], w_gate[l], w_up[l], w_down[l], ln2_g[l], ln2_b[l], alpha)
    return x
```

```python
import functools

import numpy as np
import jax
import jax.numpy as jnp
from jax import lax
from jax.experimental import pallas as pl
from jax.experimental.pallas import tpu as pltpu

F32 = jnp.float32
BF16 = jnp.bfloat16

HEAD_DIM = 64
N_HEADS = 8
D_GRP = N_HEADS * HEAD_DIM
CHUNK = 64
LEFT_CHUNKS = 8
REL_CLIP = 128
N_GROUPS = 4
EXPERTS_PER_GROUP = 8
N_EXPERTS = N_GROUPS * EXPERTS_PER_GROUP
LN_EPS = 1e-5
NEG_BIG = -1e30
LOG2E = float(np.log2(np.e))

SUBLANES = 8
LANES = 128
BLK = 256
ROWGROUPS = BLK // SUBLANES
VMEM_LIMIT = 56 * 1024 * 1024


def _nt_dot(a, b):
    return lax.dot_general(a, b, (((1,), (1,)), ((), ())), preferred_element_type=F32)


def _tn_dot(a, b):
    return lax.dot_general(a, b, (((0,), (0,)), ((), ())), preferred_element_type=F32)


def _proj_kernel(x_ref, perm_ref, wnat_ref, wva_ref, wka_ref, wkc_ref,
                 qa_ref, qc_ref, vc_ref, va_ref, ka_ref, kc_ref):
    xb = x_ref[0].astype(BF16)
    xp = jnp.dot(perm_ref[...], xb, preferred_element_type=F32).astype(BF16)
    nat = _nt_dot(wnat_ref[...], xb)
    qa_ref[0] = nat[0:D_GRP].astype(BF16)
    qc_ref[0] = nat[D_GRP:2 * D_GRP].astype(BF16)
    vc = nat[2 * D_GRP:3 * D_GRP].astype(BF16)
    va = _nt_dot(wva_ref[...], xp).astype(BF16)
    ka = jnp.dot(xp, wka_ref[...], preferred_element_type=F32).astype(BF16)
    kc = jnp.dot(xb, wkc_ref[...], preferred_element_type=F32).astype(BF16)
    for h in range(N_HEADS):
        sl = slice(h * HEAD_DIM, (h + 1) * HEAD_DIM)
        vc_ref[0, h, 0] = vc[sl]
        va_ref[0, h, 0] = va[sl]
        ka_ref[0, h, 0] = ka[:, sl]
        kc_ref[0, h, 0] = kc[:, sl]


def _key_permutation():
    r = np.arange(BLK)
    src = (r % SUBLANES) * ROWGROUPS + (ROWGROUPS - 1 - r // SUBLANES)
    return jnp.asarray(np.eye(BLK, dtype=np.float32)[src], dtype=BF16)


def _proj_call(x, wnat, wva, wka, wkc):
    b, s, d = x.shape
    nb = s // BLK
    perm = _key_permutation()
    qspec = pl.BlockSpec((1, D_GRP, BLK), lambda bi, t: (bi, 0, t))
    vspec = pl.BlockSpec((1, N_HEADS, 1, HEAD_DIM, BLK), lambda bi, t: (bi, 0, t, 0, 0))
    kspec = pl.BlockSpec((1, N_HEADS, 1, BLK, HEAD_DIM), lambda bi, t: (bi, 0, t, 0, 0))
    qshape = jax.ShapeDtypeStruct((b, D_GRP, s), BF16)
    vshape = jax.ShapeDtypeStruct((b, N_HEADS, nb, HEAD_DIM, BLK), BF16)
    kshape = jax.ShapeDtypeStruct((b, N_HEADS, nb, BLK, HEAD_DIM), BF16)
    full = lambda a: pl.BlockSpec(a.shape, lambda bi, t: (0,) * a.ndim)
    return pl.pallas_call(
        _proj_kernel,
        grid=(b, nb),
        in_specs=[pl.BlockSpec((1, BLK, d), lambda bi, t: (bi, t, 0)),
                  full(perm), full(wnat), full(wva), full(wka), full(wkc)],
        out_specs=[qspec, qspec, vspec, vspec, kspec, kspec],
        out_shape=[qshape, qshape, vshape, vshape, kshape, kshape],
        compiler_params=pltpu.CompilerParams(
            dimension_semantics=("parallel", "parallel"), vmem_limit_bytes=VMEM_LIMIT),
        name="proj",
    )(x, perm, wnat, wva, wka, wkc)


def _sublane_shift_up(t, d, sub):
    rolled = pltpu.roll(t, SUBLANES - d, axis=0)
    return jnp.where(sub < SUBLANES - d, rolled, 0.0)


def _stick_weights(z_ref, carry, diag_cols):
    n = z_ref.shape[1]
    sub = lax.broadcasted_iota(jnp.int32, (SUBLANES, n), 0)
    run = jnp.zeros((SUBLANES, n), F32)
    u = [None] * ROWGROUPS
    masks = [None] * ROWGROUPS
    for i in range(ROWGROUPS):
        zi = z_ref[i * SUBLANES:(i + 1) * SUBLANES, :]
        neg_abs = pltpu.bitcast(pltpu.bitcast(zi, jnp.uint32) | jnp.uint32(0x80000000), F32)
        sp = jnp.maximum(zi, 0.0) + jnp.log2(1.0 + jnp.exp2(neg_abs))
        if diag_cols is not None:
            masks[i] = diag_cols > -i
            sp = jnp.where(masks[i], sp, 0.0)
        run = run + sp
        u[i] = zi - run
    s1 = run + _sublane_shift_up(run, 1, sub)
    s2 = s1 + _sublane_shift_up(s1, 2, sub)
    s3 = s2 + _sublane_shift_up(s2, 4, sub)
    offset = (s3 - run) + carry
    new_carry = jnp.broadcast_to(s3[0:1, :], (SUBLANES, n)) + carry
    w_rows = []
    for i in range(ROWGROUPS):
        w = jnp.exp2(u[i] - offset)
        if diag_cols is not None:
            w = jnp.where(masks[i], w, 0.0)
        w_rows.append(w)
    return jnp.concatenate(w_rows, axis=0).astype(BF16), new_carry


def _stick_kernel(q_ref, k_ref, v_ref, o_ref, z_scr, w_scr):
    qi = pl.program_id(2)
    q_t = q_ref[0]
    sub = lax.broadcasted_iota(jnp.int32, (SUBLANES, BLK), 0)
    col = lax.broadcasted_iota(jnp.int32, (SUBLANES, BLK), 1)
    diag_cols = col - ROWGROUPS * sub - (ROWGROUPS - 1)

    def logits(kb):
        return jnp.dot(k_ref[0, 0, kb], q_t, preferred_element_type=F32) * LOG2E

    z_scr[...] = logits(qi)
    w, carry = _stick_weights(z_scr, jnp.zeros((SUBLANES, BLK), F32), diag_cols)
    w_scr[...] = w
    z_scr[...] = logits(jnp.maximum(qi - 1, 0))

    def body(t, state):
        acc, carry = state
        kb = qi - 1 - t
        pv = jnp.dot(v_ref[0, 0, kb + 1], w_scr[...], preferred_element_type=F32)
        w, carry = _stick_weights(z_scr, carry, None)
        z_scr[...] = logits(jnp.maximum(kb - 1, 0))
        w_scr[...] = w
        return acc + pv, carry

    acc, _ = lax.fori_loop(0, qi, body, (jnp.zeros((HEAD_DIM, BLK), F32), carry))
    acc = acc + jnp.dot(v_ref[0, 0, 0], w_scr[...], preferred_element_type=F32)
    o_ref[0] = acc.astype(BF16)


def _stick_call(q_t, k, v_t):
    b, _, s = q_t.shape
    nb = s // BLK
    return pl.pallas_call(
        _stick_kernel,
        grid=(b, N_HEADS, nb),
        in_specs=[pl.BlockSpec((1, HEAD_DIM, BLK), lambda bi, h, qi: (bi, h, qi)),
                  pl.BlockSpec((1, 1, nb, BLK, HEAD_DIM), lambda bi, h, qi: (bi, h, 0, 0, 0)),
                  pl.BlockSpec((1, 1, nb, HEAD_DIM, BLK), lambda bi, h, qi: (bi, h, 0, 0, 0))],
        out_specs=pl.BlockSpec((1, HEAD_DIM, BLK), lambda bi, h, qi: (bi, h, qi)),
        out_shape=jax.ShapeDtypeStruct((b, D_GRP, s), BF16),
        scratch_shapes=[pltpu.VMEM((BLK, BLK), F32), pltpu.VMEM((BLK, BLK), BF16)],
        compiler_params=pltpu.CompilerParams(
            dimension_semantics=("parallel", "parallel", "arbitrary"), vmem_limit_bytes=VMEM_LIMIT),
        name="stick",
    )(q_t, k, v_t)


WIN_BLOCKS = LEFT_CHUNKS * CHUNK // BLK + 1
TABLE_ROWS = (2 * WIN_BLOCKS - 1) * BLK


def _chunk_kernel(q_ref, k_ref, v_ref, bm_ref, o_ref):
    qi = pl.program_id(2)
    kb0 = jnp.maximum(qi - (WIN_BLOCKS - 1), 0)
    row0 = (WIN_BLOCKS - 1 - (qi - kb0)) * BLK
    q_t = q_ref[0]
    scores = []
    for j in range(WIN_BLOCKS):
        start = pl.multiple_of(row0 + j * BLK, BLK)
        sc = jnp.dot(k_ref[0, 0, kb0 + j], q_t, preferred_element_type=F32)
        scores.append(sc + bm_ref[0, pl.ds(start, BLK), :])
    m = jnp.max(scores[0], axis=0, keepdims=True)
    for sc in scores[1:]:
        m = jnp.maximum(m, jnp.max(sc, axis=0, keepdims=True))
    denom = jnp.zeros((1, BLK), F32)
    out = jnp.zeros((HEAD_DIM, BLK), F32)
    for j in range(WIN_BLOCKS):
        p = jnp.exp(scores[j] - m)
        denom = denom + jnp.sum(p, axis=0, keepdims=True)
        out = out + jnp.dot(v_ref[0, 0, kb0 + j], p.astype(BF16), preferred_element_type=F32)
    o_ref[0] = (out / denom).astype(BF16)


def _chunk_call(q_t, k, v_t, bias_mask):
    b, _, s = q_t.shape
    nb = s // BLK
    return pl.pallas_call(
        _chunk_kernel,
        grid=(b, N_HEADS, nb),
        in_specs=[pl.BlockSpec((1, HEAD_DIM, BLK), lambda bi, h, qi: (bi, h, qi)),
                  pl.BlockSpec((1, 1, nb, BLK, HEAD_DIM), lambda bi, h, qi: (bi, h, 0, 0, 0)),
                  pl.BlockSpec((1, 1, nb, HEAD_DIM, BLK), lambda bi, h, qi: (bi, h, 0, 0, 0)),
                  pl.BlockSpec((1, TABLE_ROWS, BLK), lambda bi, h, qi: (h, 0, 0))],
        out_specs=pl.BlockSpec((1, HEAD_DIM, BLK), lambda bi, h, qi: (bi, h, qi)),
        out_shape=jax.ShapeDtypeStruct((b, D_GRP, s), BF16),
        compiler_params=pltpu.CompilerParams(
            dimension_semantics=("parallel", "parallel", "arbitrary"), vmem_limit_bytes=VMEM_LIMIT),
        name="chunk",
    )(q_t, k, v_t, bias_mask)


def _bias_mask_table(rel_bias):
    shift = (WIN_BLOCKS - 1) * BLK
    key = np.arange(TABLE_ROWS)[:, None] - shift
    qry = np.arange(BLK)[None, :]
    dchunk = qry // CHUNK - np.floor_divide(key, CHUNK)
    visible = (dchunk >= 0) & (dchunk <= LEFT_CHUNKS)
    period = TABLE_ROWS + BLK
    m = np.arange(period)
    rel = np.where(m <= BLK, m, m - period) + shift
    line = rel_bias[:, np.clip(rel, -REL_CLIP, REL_CLIP) + REL_CLIP].astype(F32)
    skew = jnp.tile(line, (1, TABLE_ROWS))[:, :TABLE_ROWS * (period - 1)]
    bias = skew.reshape(-1, TABLE_ROWS, period - 1)[:, :, :BLK]
    return jnp.where(visible[None], bias, NEG_BIG)


def _layer_norm(v, g, b):
    mu = jnp.mean(v, axis=-1, keepdims=True)
    c = v - mu
    var = jnp.mean(c * c, axis=-1, keepdims=True)
    return c * lax.rsqrt(var + LN_EPS) * g + b


def _post_kernel(alpha, ya_ref, yc_ref, x_ref, woa_ref, woc_ref, g_ref, b_ref, wr_ref, br_ref,
                 h_ref, hb_ref, cw_ref):
    mix = _tn_dot(ya_ref[0], woa_ref[...]) + _tn_dot(yc_ref[0], woc_ref[...])
    h = _layer_norm(alpha * x_ref[0] + mix, g_ref[...], b_ref[...])
    h_ref[...] = h
    hb_ref[...] = h.astype(BF16)

    logits = jnp.dot(h, wr_ref[...], preferred_element_type=F32,
                     precision=lax.Precision.HIGHEST) + br_ref[...]
    lane = lax.broadcasted_iota(jnp.int32, logits.shape, 1)
    big = jnp.int32(LANES)
    is_g = (lane >= N_EXPERTS) & (lane < N_EXPERTS + N_GROUPS)
    gl = jnp.where(is_g, logits, NEG_BIG)
    gmax = jnp.max(gl, axis=-1, keepdims=True)
    g_gate = 1.0 / jnp.sum(jnp.exp(gl - gmax), axis=-1, keepdims=True)
    g_idx = jnp.min(jnp.where(gl == gmax, lane, big), axis=-1, keepdims=True) - N_EXPERTS
    in_grp = (lane >= g_idx * EXPERTS_PER_GROUP) & (lane < (g_idx + 1) * EXPERTS_PER_GROUP)
    el = jnp.where(in_grp, logits, NEG_BIG)
    e1 = jnp.max(el, axis=-1, keepdims=True)
    i1 = jnp.min(jnp.where(el == e1, lane, big), axis=-1, keepdims=True)
    el2 = jnp.where(lane == i1, NEG_BIG, el)
    e2 = jnp.max(el2, axis=-1, keepdims=True)
    i2 = jnp.min(jnp.where(el2 == e2, lane, big), axis=-1, keepdims=True)
    r = jnp.exp(e2 - e1)
    gate1 = g_gate / (1.0 + r)
    gate2 = gate1 * r
    cw_ref[...] = jnp.where(lane == i1, gate1, jnp.where(lane == i2, gate2, 0.0))


def _post_call(ya_t, yc_t, x, woa, woc, g, b, wr, br, alpha):
    bsz, s, d = x.shape
    nb = s // BLK
    n = bsz * s
    row = lambda bi, t: (bi * nb + t, 0)
    full = lambda a: pl.BlockSpec(a.shape, lambda bi, t: (0,) * a.ndim)
    return pl.pallas_call(
        functools.partial(_post_kernel, alpha),
        grid=(bsz, nb),
        in_specs=[pl.BlockSpec((1, D_GRP, BLK), lambda bi, t: (bi, 0, t)),
                  pl.BlockSpec((1, D_GRP, BLK), lambda bi, t: (bi, 0, t)),
                  pl.BlockSpec((1, BLK, d), lambda bi, t: (bi, t, 0)),
                  full(woa), full(woc), full(g), full(b), full(wr), full(br)],
        out_specs=[pl.BlockSpec((BLK, d), row), pl.BlockSpec((BLK, d), row),
                   pl.BlockSpec((BLK, LANES), row)],
        out_shape=[jax.ShapeDtypeStruct((n, d), F32), jax.ShapeDtypeStruct((n, d), BF16),
                   jax.ShapeDtypeStruct((n, LANES), F32)],
        compiler_params=pltpu.CompilerParams(
            dimension_semantics=("parallel", "parallel"), vmem_limit_bytes=VMEM_LIMIT),
        name="post",
    )(ya_t, yc_t, x, woa, woc, g, b, wr, br)


MOE_TM = 512


def _moe_kernel(alpha, hb_ref, h_ref, cw_ref, wg_ref, wu_ref, wd_ref, g_ref, b_ref, o_ref, acc_ref):
    e = pl.program_id(1)

    @pl.when(e == 0)
    def _():
        acc_ref[...] = jnp.zeros_like(acc_ref)

    hb = hb_ref[...]
    gate = jnp.dot(hb, wg_ref[0], preferred_element_type=F32)
    up = jnp.dot(hb, wu_ref[0], preferred_element_type=F32)
    lane = lax.broadcasted_iota(jnp.int32, cw_ref.shape, 1)
    cw = jnp.sum(jnp.where(lane == e, cw_ref[...], 0.0), axis=-1, keepdims=True)
    hid = gate * (1.0 / (1.0 + jnp.exp(-gate))) * up * cw
    acc_ref[...] += jnp.dot(hid.astype(BF16), wd_ref[0], preferred_element_type=F32)

    @pl.when(e == pl.num_programs(1) - 1)
    def _():
        o_ref[...] = _layer_norm(alpha * h_ref[...] + acc_ref[...], g_ref[...], b_ref[...])


def _moe_call(hb, h, cw, wg, wu, wd, g, b, alpha):
    n, d = h.shape
    ne, _, df = wg.shape
    tok = lambda t, e: (t, 0)
    full = lambda a: pl.BlockSpec(a.shape, lambda t, e: (0,) * a.ndim)
    return pl.pallas_call(
        functools.partial(_moe_kernel, alpha),
        grid=(n // MOE_TM, ne),
        in_specs=[pl.BlockSpec((MOE_TM, d), tok), pl.BlockSpec((MOE_TM, d), tok),
                  pl.BlockSpec((MOE_TM, LANES), tok),
                  pl.BlockSpec((1, d, df), lambda t, e: (e, 0, 0)),
                  pl.BlockSpec((1, d, df), lambda t, e: (e, 0, 0)),
                  pl.BlockSpec((1, df, d), lambda t, e: (e, 0, 0)),
                  full(g), full(b)],
        out_specs=pl.BlockSpec((MOE_TM, d), tok),
        out_shape=jax.ShapeDtypeStruct((n, d), F32),
        scratch_shapes=[pltpu.VMEM((MOE_TM, d), F32)],
        compiler_params=pltpu.CompilerParams(
            dimension_semantics=("parallel", "arbitrary"), vmem_limit_bytes=VMEM_LIMIT),
        name="moe",
    )(hb, h, cw, wg, wu, wd, g, b)


def _layer(x, w_in, w_out, rel_bias, ln1_g, ln1_b, w_group, b_group, w_expert, b_expert,
           w_gate, w_up, w_down, ln2_g, ln2_b, alpha):
    bsz, s, d = x.shape
    scale = HEAD_DIM ** -0.5
    wqa, wka, wva, wqc, wkc, wvc = [w_in[:, i * D_GRP:(i + 1) * D_GRP] for i in range(6)]
    wnat = jnp.concatenate([wqa * scale, wqc * scale, wvc], axis=1).T.astype(BF16)
    qa_t, qc_t, vc_t, va_t, ka, kc = _proj_call(
        x, wnat, wva.T.astype(BF16), wka.astype(BF16), wkc.astype(BF16))

    ya_t = _stick_call(qa_t, ka, va_t)
    yc_t = _chunk_call(qc_t, kc, vc_t, _bias_mask_table(rel_bias))

    wo = w_out.astype(BF16)
    wr = jnp.zeros((d, LANES), F32)
    wr = wr.at[:, :N_EXPERTS].set(w_expert.transpose(1, 0, 2).reshape(d, N_EXPERTS))
    wr = wr.at[:, N_EXPERTS:N_EXPERTS + N_GROUPS].set(w_group)
    br = jnp.zeros((1, LANES), F32)
    br = br.at[0, :N_EXPERTS].set(b_expert.reshape(N_EXPERTS))
    br = br.at[0, N_EXPERTS:N_EXPERTS + N_GROUPS].set(b_group)
    h, hb, cw = _post_call(ya_t, yc_t, x, wo[:D_GRP], wo[D_GRP:], ln1_g[None], ln1_b[None],
                           wr, br, alpha)

    df = w_gate.shape[-1]
    out = _moe_call(hb, h, cw,
                    w_gate.reshape(N_EXPERTS, d, df).astype(BF16),
                    w_up.reshape(N_EXPERTS, d, df).astype(BF16),
                    w_down.reshape(N_EXPERTS, df, d).astype(BF16),
                    ln2_g[None], ln2_b[None], alpha)
    return out.reshape(bsz, s, d)


def kernel(x, w_in, w_out, rel_bias, ln1_g, ln1_b, w_group, b_group, w_expert, b_expert,
           w_gate, w_up, w_down, ln2_g, ln2_b):
    depth = w_in.shape[0]
    alpha = (2.0 * depth) ** 0.25
    for l in range(depth):
        x = _layer(x, w_in[l], w_out[l], rel_bias[l], ln1_g[l], ln1_b[l], w_group[l], b_group[l],
                   w_expert[l], b_expert[l], w_gate[l], w_up[l], w_down[l], ln2_g[l], ln2_b[l], alpha)
    return x
```

```python
import functools

import numpy as np
import jax
import jax.numpy as jnp
from jax import lax
from jax.experimental import pallas as pl
from jax.experimental.pallas import tpu as pltpu

F32 = jnp.float32
BF16 = jnp.bfloat16

HEAD_DIM = 64
N_HEADS = 8
D_GRP = N_HEADS * HEAD_DIM
CHUNK = 64
LEFT_CHUNKS = 8
REL_CLIP = 128
N_GROUPS = 4
EXPERTS_PER_GROUP = 8
N_EXPERTS = N_GROUPS * EXPERTS_PER_GROUP
LN_EPS = 1e-5
NEG_BIG = -1e30
LOG2E = float(np.log2(np.e))

SUBLANES = 8
LANES = 128
BLK = 256
ROWGROUPS = BLK // SUBLANES
VMEM_LIMIT = 56 * 1024 * 1024


def _nt_dot(a, b):
    return lax.dot_general(a, b, (((1,), (1,)), ((), ())), preferred_element_type=F32)


def _tn_dot(a, b):
    return lax.dot_general(a, b, (((0,), (0,)), ((), ())), preferred_element_type=F32)


def _proj_kernel(x_ref, perm_ref, wnat_ref, wva_ref, wka_ref, wkc_ref,
                 qa_ref, qc_ref, vc_ref, va_ref, ka_ref, kc_ref):
    xb = x_ref[0].astype(BF16)
    xp = jnp.dot(perm_ref[...], xb, preferred_element_type=F32).astype(BF16)
    nat = _nt_dot(wnat_ref[...], xb)
    qa_ref[0] = nat[0:D_GRP].astype(BF16)
    qc_ref[0] = nat[D_GRP:2 * D_GRP].astype(BF16)
    vc = nat[2 * D_GRP:3 * D_GRP].astype(BF16)
    va = _nt_dot(wva_ref[...], xp).astype(BF16)
    ka = jnp.dot(xp, wka_ref[...], preferred_element_type=F32).astype(BF16)
    kc = jnp.dot(xb, wkc_ref[...], preferred_element_type=F32).astype(BF16)
    for h in range(N_HEADS):
        sl = slice(h * HEAD_DIM, (h + 1) * HEAD_DIM)
        vc_ref[0, h, 0] = vc[sl]
        va_ref[0, h, 0] = va[sl]
        ka_ref[0, h, 0] = ka[:, sl]
        kc_ref[0, h, 0] = kc[:, sl]


def _key_permutation():
    r = np.arange(BLK)
    src = (r % SUBLANES) * ROWGROUPS + (ROWGROUPS - 1 - r // SUBLANES)
    return jnp.asarray(np.eye(BLK, dtype=np.float32)[src], dtype=BF16)


def _proj_call(x, wnat, wva, wka, wkc):
    b, s, d = x.shape
    nb = s // BLK
    perm = _key_permutation()
    qspec = pl.BlockSpec((1, D_GRP, BLK), lambda bi, t: (bi, 0, t))
    vspec = pl.BlockSpec((1, N_HEADS, 1, HEAD_DIM, BLK), lambda bi, t: (bi, 0, t, 0, 0))
    kspec = pl.BlockSpec((1, N_HEADS, 1, BLK, HEAD_DIM), lambda bi, t: (bi, 0, t, 0, 0))
    qshape = jax.ShapeDtypeStruct((b, D_GRP, s), BF16)
    vshape = jax.ShapeDtypeStruct((b, N_HEADS, nb, HEAD_DIM, BLK), BF16)
    kshape = jax.ShapeDtypeStruct((b, N_HEADS, nb, BLK, HEAD_DIM), BF16)
    full = lambda a: pl.BlockSpec(a.shape, lambda bi, t: (0,) * a.ndim)
    return pl.pallas_call(
        _proj_kernel,
        grid=(b, nb),
        in_specs=[pl.BlockSpec((1, BLK, d), lambda bi, t: (bi, t, 0)),
                  full(perm), full(wnat), full(wva), full(wka), full(wkc)],
        out_specs=[qspec, qspec, vspec, vspec, kspec, kspec],
        out_shape=[qshape, qshape, vshape, vshape, kshape, kshape],
        compiler_params=pltpu.CompilerParams(
            dimension_semantics=("parallel", "parallel"), vmem_limit_bytes=VMEM_LIMIT),
        name="proj",
    )(x, perm, wnat, wva, wka, wkc)


def _sublane_shift_up(t, d, sub):
    rolled = pltpu.roll(t, SUBLANES - d, axis=0)
    return jnp.where(sub < SUBLANES - d, rolled, 0.0)


def _stick_weights(z_ref, carry, diag_cols):
    n = z_ref.shape[1]
    sub = lax.broadcasted_iota(jnp.int32, (SUBLANES, n), 0)
    run = jnp.zeros((SUBLANES, n), F32)
    u = [None] * ROWGROUPS
    masks = [None] * ROWGROUPS
    for i in range(ROWGROUPS):
        zi = z_ref[i * SUBLANES:(i + 1) * SUBLANES, :]
        sp = jnp.maximum(zi, 0.0) + jnp.log2(1.0 + jnp.exp2(-jnp.abs(zi)))
        if diag_cols is not None:
            masks[i] = diag_cols > -i
            sp = jnp.where(masks[i], sp, 0.0)
        run = run + sp
        u[i] = zi - run
    s1 = run + _sublane_shift_up(run, 1, sub)
    s2 = s1 + _sublane_shift_up(s1, 2, sub)
    s3 = s2 + _sublane_shift_up(s2, 4, sub)
    offset = (s3 - run) + carry
    new_carry = jnp.broadcast_to(s3[0:1, :], (SUBLANES, n)) + carry
    w_rows = []
    for i in range(ROWGROUPS):
        w = jnp.exp2(u[i] - offset)
        if diag_cols is not None:
            w = jnp.where(masks[i], w, 0.0)
        w_rows.append(w)
    return jnp.concatenate(w_rows, axis=0).astype(BF16), new_carry


def _stick_kernel(q_ref, k_ref, v_ref, o_ref, z_scr, w_scr):
    qi = pl.program_id(2)
    q_t = q_ref[0]
    sub = lax.broadcasted_iota(jnp.int32, (SUBLANES, BLK), 0)
    col = lax.broadcasted_iota(jnp.int32, (SUBLANES, BLK), 1)
    diag_cols = col - ROWGROUPS * sub - (ROWGROUPS - 1)

    def logits(kb):
        return jnp.dot(k_ref[0, 0, kb], q_t, preferred_element_type=F32) * LOG2E

    z_scr[...] = logits(qi)
    w, carry = _stick_weights(z_scr, jnp.zeros((SUBLANES, BLK), F32), diag_cols)
    w_scr[...] = w
    z_scr[...] = logits(jnp.maximum(qi - 1, 0))

    def body(t, state):
        acc, carry = state
        kb = qi - 1 - t
        pv = jnp.dot(v_ref[0, 0, kb + 1], w_scr[...], preferred_element_type=F32)
        w, carry = _stick_weights(z_scr, carry, None)
        z_scr[...] = logits(jnp.maximum(kb - 1, 0))
        w_scr[...] = w
        return acc + pv, carry

    acc, _ = lax.fori_loop(0, qi, body, (jnp.zeros((HEAD_DIM, BLK), F32), carry))
    acc = acc + jnp.dot(v_ref[0, 0, 0], w_scr[...], preferred_element_type=F32)
    o_ref[0] = acc.astype(BF16)


def _stick_call(q_t, k, v_t):
    b, _, s = q_t.shape
    nb = s // BLK
    return pl.pallas_call(
        _stick_kernel,
        grid=(b, N_HEADS, nb),
        in_specs=[pl.BlockSpec((1, HEAD_DIM, BLK), lambda bi, h, qi: (bi, h, qi)),
                  pl.BlockSpec((1, 1, nb, BLK, HEAD_DIM), lambda bi, h, qi: (bi, h, 0, 0, 0)),
                  pl.BlockSpec((1, 1, nb, HEAD_DIM, BLK), lambda bi, h, qi: (bi, h, 0, 0, 0))],
        out_specs=pl.BlockSpec((1, HEAD_DIM, BLK), lambda bi, h, qi: (bi, h, qi)),
        out_shape=jax.ShapeDtypeStruct((b, D_GRP, s), BF16),
        scratch_shapes=[pltpu.VMEM((BLK, BLK), F32), pltpu.VMEM((BLK, BLK), BF16)],
        compiler_params=pltpu.CompilerParams(
            dimension_semantics=("parallel", "parallel", "arbitrary"), vmem_limit_bytes=VMEM_LIMIT),
        name="stick",
    )(q_t, k, v_t)


WIN_BLOCKS = LEFT_CHUNKS * CHUNK // BLK + 1
TABLE_ROWS = (2 * WIN_BLOCKS - 1) * BLK


def _chunk_kernel(q_ref, k_ref, v_ref, lines_ref, o_ref, bm_ref):
    qi = pl.program_id(2)

    @pl.when(qi == 0)
    def _():
        row = lax.broadcasted_iota(jnp.int32, (BLK, BLK), 0)
        qchunk = lax.broadcasted_iota(jnp.int32, (BLK, BLK), 1) // CHUNK
        for rb in range(TABLE_ROWS // BLK):
            line = jnp.broadcast_to(lines_ref[0, rb:rb + 1, :], (BLK, 2 * BLK))
            bias = pltpu.roll(line, 0, 1, stride=1, stride_axis=0)[:, :BLK]
            kchunk = (row + (rb - (WIN_BLOCKS - 1)) * BLK) >> 6
            dchunk = qchunk - kchunk
            visible = (dchunk >= 0) & (dchunk <= LEFT_CHUNKS)
            bm_ref[rb * BLK:(rb + 1) * BLK, :] = jnp.where(visible, bias, NEG_BIG)

    kb0 = jnp.maximum(qi - (WIN_BLOCKS - 1), 0)
    row0 = (WIN_BLOCKS - 1 - (qi - kb0)) * BLK
    q_t = q_ref[0]
    scores = []
    for j in range(WIN_BLOCKS):
        start = pl.multiple_of(row0 + j * BLK, BLK)
        sc = jnp.dot(k_ref[0, 0, kb0 + j], q_t, preferred_element_type=F32)
        scores.append(sc + bm_ref[pl.ds(start, BLK), :])
    m = jnp.max(scores[0], axis=0, keepdims=True)
    for sc in scores[1:]:
        m = jnp.maximum(m, jnp.max(sc, axis=0, keepdims=True))
    denom = jnp.zeros((1, BLK), F32)
    out = jnp.zeros((HEAD_DIM, BLK), F32)
    for j in range(WIN_BLOCKS):
        p = jnp.exp(scores[j] - m)
        denom = denom + jnp.sum(p, axis=0, keepdims=True)
        out = out + jnp.dot(v_ref[0, 0, kb0 + j], p.astype(BF16), preferred_element_type=F32)
    o_ref[0] = (out / denom).astype(BF16)


def _chunk_call(q_t, k, v_t, rel_bias):
    b, _, s = q_t.shape
    nb = s // BLK
    lines = _bias_lines(rel_bias)
    return pl.pallas_call(
        _chunk_kernel,
        grid=(b, N_HEADS, nb),
        in_specs=[pl.BlockSpec((1, HEAD_DIM, BLK), lambda bi, h, qi: (bi, h, qi)),
                  pl.BlockSpec((1, 1, nb, BLK, HEAD_DIM), lambda bi, h, qi: (bi, h, 0, 0, 0)),
                  pl.BlockSpec((1, 1, nb, HEAD_DIM, BLK), lambda bi, h, qi: (bi, h, 0, 0, 0)),
                  pl.BlockSpec((1,) + lines.shape[1:], lambda bi, h, qi: (h, 0, 0))],
        out_specs=pl.BlockSpec((1, HEAD_DIM, BLK), lambda bi, h, qi: (bi, h, qi)),
        out_shape=jax.ShapeDtypeStruct((b, D_GRP, s), BF16),
        scratch_shapes=[pltpu.VMEM((TABLE_ROWS, BLK), F32)],
        compiler_params=pltpu.CompilerParams(
            dimension_semantics=("parallel", "parallel", "arbitrary"), vmem_limit_bytes=VMEM_LIMIT),
        name="chunk",
    )(q_t, k, v_t, lines)


def _bias_lines(rel_bias):
    m = np.arange(2 * BLK)[None, :]
    diff = np.where(m < BLK, m, m - 2 * BLK)
    rel = diff + ((WIN_BLOCKS - 1) - np.arange(TABLE_ROWS // BLK)[:, None]) * BLK
    return rel_bias[:, np.clip(rel, -REL_CLIP, REL_CLIP) + REL_CLIP].astype(F32)


def _layer_norm(v, g, b):
    mu = jnp.mean(v, axis=-1, keepdims=True)
    c = v - mu
    var = jnp.mean(c * c, axis=-1, keepdims=True)
    return c * lax.rsqrt(var + LN_EPS) * g + b


def _post_kernel(alpha, ya_ref, yc_ref, x_ref, woa_ref, woc_ref, g_ref, b_ref, wr_ref, br_ref,
                 h_ref, hb_ref, cw_ref):
    mix = _tn_dot(ya_ref[0], woa_ref[...]) + _tn_dot(yc_ref[0], woc_ref[...])
    h = _layer_norm(alpha * x_ref[0] + mix, g_ref[...], b_ref[...])
    h_ref[...] = h
    hb_ref[...] = h.astype(BF16)

    logits = jnp.dot(h, wr_ref[...], preferred_element_type=F32,
                     precision=lax.Precision.HIGHEST) + br_ref[...]
    lane = lax.broadcasted_iota(jnp.int32, logits.shape, 1)
    big = jnp.int32(LANES)
    is_g = (lane >= N_EXPERTS) & (lane < N_EXPERTS + N_GROUPS)
    gl = jnp.where(is_g, logits, NEG_BIG)
    gmax = jnp.max(gl, axis=-1, keepdims=True)
    g_gate = 1.0 / jnp.sum(jnp.exp(gl - gmax), axis=-1, keepdims=True)
    g_idx = jnp.min(jnp.where(gl == gmax, lane, big), axis=-1, keepdims=True) - N_EXPERTS
    in_grp = (lane >= g_idx * EXPERTS_PER_GROUP) & (lane < (g_idx + 1) * EXPERTS_PER_GROUP)
    el = jnp.where(in_grp, logits, NEG_BIG)
    e1 = jnp.max(el, axis=-1, keepdims=True)
    i1 = jnp.min(jnp.where(el == e1, lane, big), axis=-1, keepdims=True)
    el2 = jnp.where(lane == i1, NEG_BIG, el)
    e2 = jnp.max(el2, axis=-1, keepdims=True)
    i2 = jnp.min(jnp.where(el2 == e2, lane, big), axis=-1, keepdims=True)
    r = jnp.exp(e2 - e1)
    gate1 = g_gate / (1.0 + r)
    gate2 = gate1 * r
    cw_ref[...] = jnp.where(lane == i1, gate1, jnp.where(lane == i2, gate2, 0.0))


def _post_call(ya_t, yc_t, x, woa, woc, g, b, wr, br, alpha):
    bsz, s, d = x.shape
    nb = s // BLK
    n = bsz * s
    row = lambda bi, t: (bi * nb + t, 0)
    full = lambda a: pl.BlockSpec(a.shape, lambda bi, t: (0,) * a.ndim)
    return pl.pallas_call(
        functools.partial(_post_kernel, alpha),
        grid=(bsz, nb),
        in_specs=[pl.BlockSpec((1, D_GRP, BLK), lambda bi, t: (bi, 0, t)),
                  pl.BlockSpec((1, D_GRP, BLK), lambda bi, t: (bi, 0, t)),
                  pl.BlockSpec((1, BLK, d), lambda bi, t: (bi, t, 0)),
                  full(woa), full(woc), full(g), full(b), full(wr), full(br)],
        out_specs=[pl.BlockSpec((BLK, d), row), pl.BlockSpec((BLK, d), row),
                   pl.BlockSpec((BLK, LANES), row)],
        out_shape=[jax.ShapeDtypeStruct((n, d), F32), jax.ShapeDtypeStruct((n, d), BF16),
                   jax.ShapeDtypeStruct((n, LANES), F32)],
        compiler_params=pltpu.CompilerParams(
            dimension_semantics=("parallel", "parallel"), vmem_limit_bytes=VMEM_LIMIT),
        name="post",
    )(ya_t, yc_t, x, woa, woc, g, b, wr, br)


MOE_TM = 512


def _moe_kernel(alpha, hb_ref, h_ref, cw_ref, wg_ref, wu_ref, wd_ref, g_ref, b_ref, o_ref, acc_ref):
    e = pl.program_id(1)

    @pl.when(e == 0)
    def _():
        acc_ref[...] = jnp.zeros_like(acc_ref)

    hb = hb_ref[...]
    gate = jnp.dot(hb, wg_ref[0], preferred_element_type=F32)
    up = jnp.dot(hb, wu_ref[0], preferred_element_type=F32)
    lane = lax.broadcasted_iota(jnp.int32, cw_ref.shape, 1)
    cw = jnp.sum(jnp.where(lane == e, cw_ref[...], 0.0), axis=-1, keepdims=True)
    hid = gate * (1.0 / (1.0 + jnp.exp(-gate))) * up * cw
    acc_ref[...] += jnp.dot(hid.astype(BF16), wd_ref[0], preferred_element_type=F32)

    @pl.when(e == pl.num_programs(1) - 1)
    def _():
        o_ref[...] = _layer_norm(alpha * h_ref[...] + acc_ref[...], g_ref[...], b_ref[...])


def _moe_call(hb, h, cw, wg, wu, wd, g, b, alpha):
    n, d = h.shape
    ne, _, df = wg.shape
    tok = lambda t, e: (t, 0)
    full = lambda a: pl.BlockSpec(a.shape, lambda t, e: (0,) * a.ndim)
    return pl.pallas_call(
        functools.partial(_moe_kernel, alpha),
        grid=(n // MOE_TM, ne),
        in_specs=[pl.BlockSpec((MOE_TM, d), tok), pl.BlockSpec((MOE_TM, d), tok),
                  pl.BlockSpec((MOE_TM, LANES), tok),
                  pl.BlockSpec((1, d, df), lambda t, e: (e, 0, 0)),
                  pl.BlockSpec((1, d, df), lambda t, e: (e, 0, 0)),
                  pl.BlockSpec((1, df, d), lambda t, e: (e, 0, 0)),
                  full(g), full(b)],
        out_specs=pl.BlockSpec((MOE_TM, d), tok),
        out_shape=jax.ShapeDtypeStruct((n, d), F32),
        scratch_shapes=[pltpu.VMEM((MOE_TM, d), F32)],
        compiler_params=pltpu.CompilerParams(
            dimension_semantics=("parallel", "arbitrary"), vmem_limit_bytes=VMEM_LIMIT),
        name="moe",
    )(hb, h, cw, wg, wu, wd, g, b)


def _layer(x, w_in, w_out, rel_bias, ln1_g, ln1_b, w_group, b_group, w_expert, b_expert,
           w_gate, w_up, w_down, ln2_g, ln2_b, alpha):
    bsz, s, d = x.shape
    scale = HEAD_DIM ** -0.5
    wqa, wka, wva, wqc, wkc, wvc = [w_in[:, i * D_GRP:(i + 1) * D_GRP] for i in range(6)]
    wnat = jnp.concatenate([wqa * scale, wqc * scale, wvc], axis=1).T.astype(BF16)
    qa_t, qc_t, vc_t, va_t, ka, kc = _proj_call(
        x, wnat, wva.T.astype(BF16), wka.astype(BF16), wkc.astype(BF16))

    ya_t = _stick_call(qa_t, ka, va_t)
    yc_t = _chunk_call(qc_t, kc, vc_t, rel_bias)

    wo = w_out.astype(BF16)
    wr = jnp.zeros((d, LANES), F32)
    wr = wr.at[:, :N_EXPERTS].set(w_expert.transpose(1, 0, 2).reshape(d, N_EXPERTS))
    wr = wr.at[:, N_EXPERTS:N_EXPERTS + N_GROUPS].set(w_group)
    br = jnp.zeros((1, LANES), F32)
    br = br.at[0, :N_EXPERTS].set(b_expert.reshape(N_EXPERTS))
    br = br.at[0, N_EXPERTS:N_EXPERTS + N_GROUPS].set(b_group)
    h, hb, cw = _post_call(ya_t, yc_t, x, wo[:D_GRP], wo[D_GRP:], ln1_g[None], ln1_b[None],
                           wr, br, alpha)

    df = w_gate.shape[-1]
    out = _moe_call(hb, h, cw,
                    w_gate.reshape(N_EXPERTS, d, df).astype(BF16),
                    w_up.reshape(N_EXPERTS, d, df).astype(BF16),
                    w_down.reshape(N_EXPERTS, df, d).astype(BF16),
                    ln2_g[None], ln2_b[None], alpha)
    return out.reshape(bsz, s, d)


def kernel(x, w_in, w_out, rel_bias, ln1_g, ln1_b, w_group, b_group, w_expert, b_expert,
           w_gate, w_up, w_down, ln2_g, ln2_b):
    depth = w_in.shape[0]
    alpha = (2.0 * depth) ** 0.25
    for l in range(depth):
        x = _layer(x, w_in[l], w_out[l], rel_bias[l], ln1_g[l], ln1_b[l], w_group[l], b_group[l],
                   w_expert[l], b_expert[l], w_gate[l], w_up[l], w_down[l], ln2_g[l], ln2_b[l], alpha)
    return x
```

```python
import functools

import numpy as np
import jax
import jax.numpy as jnp
from jax import lax
from jax.experimental import pallas as pl
from jax.experimental.pallas import tpu as pltpu

F32 = jnp.float32
BF16 = jnp.bfloat16

HEAD_DIM = 64
N_HEADS = 8
D_GRP = N_HEADS * HEAD_DIM
CHUNK = 64
LEFT_CHUNKS = 8
REL_CLIP = 128
N_GROUPS = 4
EXPERTS_PER_GROUP = 8
N_EXPERTS = N_GROUPS * EXPERTS_PER_GROUP
LN_EPS = 1e-5
NEG_BIG = -1e30
LOG2E = float(np.log2(np.e))

SUBLANES = 8
LANES = 128
BLK = 256
ROWGROUPS = BLK // SUBLANES
VMEM_LIMIT = 56 * 1024 * 1024


def _nt_dot(a, b):
    return lax.dot_general(a, b, (((1,), (1,)), ((), ())), preferred_element_type=F32)


def _tn_dot(a, b):
    return lax.dot_general(a, b, (((0,), (0,)), ((), ())), preferred_element_type=F32)


def _proj_kernel(x_ref, perm_ref, wnat_ref, wva_ref, wka_ref, wkc_ref,
                 qa_ref, qc_ref, vc_ref, va_ref, ka_ref, kc_ref):
    xb = x_ref[0].astype(BF16)
    xp = jnp.dot(perm_ref[...], xb, preferred_element_type=F32).astype(BF16)
    nat = _nt_dot(wnat_ref[...], xb)
    qa_ref[0] = nat[0:D_GRP].astype(BF16)
    qc_ref[0] = nat[D_GRP:2 * D_GRP].astype(BF16)
    vc = nat[2 * D_GRP:3 * D_GRP].astype(BF16)
    va = _nt_dot(wva_ref[...], xp).astype(BF16)
    ka = jnp.dot(xp, wka_ref[...], preferred_element_type=F32).astype(BF16)
    kc = jnp.dot(xb, wkc_ref[...], preferred_element_type=F32).astype(BF16)
    for h in range(N_HEADS):
        sl = slice(h * HEAD_DIM, (h + 1) * HEAD_DIM)
        vc_ref[0, h, 0] = vc[sl]
        va_ref[0, h, 0] = va[sl]
        ka_ref[0, h, 0] = ka[:, sl]
        kc_ref[0, h, 0] = kc[:, sl]


def _key_permutation():
    r = np.arange(BLK)
    src = (r % SUBLANES) * ROWGROUPS + (ROWGROUPS - 1 - r // SUBLANES)
    return jnp.asarray(np.eye(BLK, dtype=np.float32)[src], dtype=BF16)


def _proj_call(x, wnat, wva, wka, wkc):
    b, s, d = x.shape
    nb = s // BLK
    perm = _key_permutation()
    qspec = pl.BlockSpec((1, D_GRP, BLK), lambda bi, t: (bi, 0, t))
    vspec = pl.BlockSpec((1, N_HEADS, 1, HEAD_DIM, BLK), lambda bi, t: (bi, 0, t, 0, 0))
    kspec = pl.BlockSpec((1, N_HEADS, 1, BLK, HEAD_DIM), lambda bi, t: (bi, 0, t, 0, 0))
    qshape = jax.ShapeDtypeStruct((b, D_GRP, s), BF16)
    vshape = jax.ShapeDtypeStruct((b, N_HEADS, nb, HEAD_DIM, BLK), BF16)
    kshape = jax.ShapeDtypeStruct((b, N_HEADS, nb, BLK, HEAD_DIM), BF16)
    full = lambda a: pl.BlockSpec(a.shape, lambda bi, t: (0,) * a.ndim)
    return pl.pallas_call(
        _proj_kernel,
        grid=(b, nb),
        in_specs=[pl.BlockSpec((1, BLK, d), lambda bi, t: (bi, t, 0)),
                  full(perm), full(wnat), full(wva), full(wka), full(wkc)],
        out_specs=[qspec, qspec, vspec, vspec, kspec, kspec],
        out_shape=[qshape, qshape, vshape, vshape, kshape, kshape],
        compiler_params=pltpu.CompilerParams(
            dimension_semantics=("parallel", "parallel"), vmem_limit_bytes=VMEM_LIMIT),
        name="proj",
    )(x, perm, wnat, wva, wka, wkc)


def _sublane_shift_up(t, d, sub):
    rolled = pltpu.roll(t, SUBLANES - d, axis=0)
    return jnp.where(sub < SUBLANES - d, rolled, 0.0)


def _stick_weights(z_ref, carry, diag_cols):
    n = z_ref.shape[1]
    sub = lax.broadcasted_iota(jnp.int32, (SUBLANES, n), 0)
    run = jnp.zeros((SUBLANES, n), F32)
    u = [None] * ROWGROUPS
    masks = [None] * ROWGROUPS
    for i in range(ROWGROUPS):
        zi = z_ref[i * SUBLANES:(i + 1) * SUBLANES, :]
        sp = jnp.maximum(zi, 0.0) + jnp.log2(1.0 + jnp.exp2(-jnp.abs(zi)))
        if diag_cols is not None:
            masks[i] = diag_cols > -i
            sp = jnp.where(masks[i], sp, 0.0)
        run = run + sp
        u[i] = zi - run
    s1 = run + _sublane_shift_up(run, 1, sub)
    s2 = s1 + _sublane_shift_up(s1, 2, sub)
    s3 = s2 + _sublane_shift_up(s2, 4, sub)
    offset = (s3 - run) + carry
    new_carry = jnp.broadcast_to(s3[0:1, :], (SUBLANES, n)) + carry
    w_rows = []
    for i in range(ROWGROUPS):
        w = jnp.exp2(u[i] - offset)
        if diag_cols is not None:
            w = jnp.where(masks[i], w, 0.0)
        w_rows.append(w)
    return jnp.concatenate(w_rows, axis=0).astype(BF16), new_carry


def _stick_kernel(q_ref, k_ref, v_ref, o_ref, z_scr, w_scr):
    qi = pl.program_id(2)
    q_t = q_ref[0]
    sub = lax.broadcasted_iota(jnp.int32, (SUBLANES, BLK), 0)
    col = lax.broadcasted_iota(jnp.int32, (SUBLANES, BLK), 1)
    diag_cols = col - ROWGROUPS * sub - (ROWGROUPS - 1)

    def logits(kb):
        return jnp.dot(k_ref[0, 0, kb], q_t, preferred_element_type=F32) * LOG2E

    z_scr[...] = logits(qi)
    w, carry = _stick_weights(z_scr, jnp.zeros((SUBLANES, BLK), F32), diag_cols)
    w_scr[...] = w
    z_scr[...] = logits(jnp.maximum(qi - 1, 0))

    def body(t, state):
        acc, carry = state
        kb = qi - 1 - t
        pv = jnp.dot(v_ref[0, 0, kb + 1], w_scr[...], preferred_element_type=F32)
        w, carry = _stick_weights(z_scr, carry, None)
        z_scr[...] = logits(jnp.maximum(kb - 1, 0))
        w_scr[...] = w
        return acc + pv, carry

    acc, _ = lax.fori_loop(0, qi, body, (jnp.zeros((HEAD_DIM, BLK), F32), carry))
    acc = acc + jnp.dot(v_ref[0, 0, 0], w_scr[...], preferred_element_type=F32)
    o_ref[0] = acc.astype(BF16)


def _stick_call(q_t, k, v_t):
    b, _, s = q_t.shape
    nb = s // BLK
    return pl.pallas_call(
        _stick_kernel,
        grid=(b, N_HEADS, nb),
        in_specs=[pl.BlockSpec((1, HEAD_DIM, BLK), lambda bi, h, qi: (bi, h, qi)),
                  pl.BlockSpec((1, 1, nb, BLK, HEAD_DIM), lambda bi, h, qi: (bi, h, 0, 0, 0)),
                  pl.BlockSpec((1, 1, nb, HEAD_DIM, BLK), lambda bi, h, qi: (bi, h, 0, 0, 0))],
        out_specs=pl.BlockSpec((1, HEAD_DIM, BLK), lambda bi, h, qi: (bi, h, qi)),
        out_shape=jax.ShapeDtypeStruct((b, D_GRP, s), BF16),
        scratch_shapes=[pltpu.VMEM((BLK, BLK), F32), pltpu.VMEM((BLK, BLK), BF16)],
        compiler_params=pltpu.CompilerParams(
            dimension_semantics=("parallel", "parallel", "arbitrary"), vmem_limit_bytes=VMEM_LIMIT),
        name="stick",
    )(q_t, k, v_t)


WIN_BLOCKS = LEFT_CHUNKS * CHUNK // BLK + 1
TABLE_ROWS = (2 * WIN_BLOCKS - 1) * BLK


def _chunk_kernel(q_ref, k_ref, v_ref, lines_ref, o_ref, bm_ref):
    qi = pl.program_id(2)

    @pl.when(qi == 0)
    def _():
        row = lax.broadcasted_iota(jnp.int32, (BLK, BLK), 0)
        qchunk = lax.broadcasted_iota(jnp.int32, (BLK, BLK), 1) // CHUNK
        for rb in range(TABLE_ROWS // BLK):
            line = jnp.broadcast_to(lines_ref[0, rb:rb + 1, :], (BLK, 2 * BLK))
            bias = pltpu.roll(line, 0, 1, stride=1, stride_axis=0)[:, :BLK]
            kchunk = (row + (rb - (WIN_BLOCKS - 1)) * BLK) >> 6
            dchunk = qchunk - kchunk
            visible = (dchunk >= 0) & (dchunk <= LEFT_CHUNKS)
            bm_ref[rb * BLK:(rb + 1) * BLK, :] = jnp.where(visible, bias, NEG_BIG)

    kb0 = jnp.maximum(qi - (WIN_BLOCKS - 1), 0)
    row0 = (WIN_BLOCKS - 1 - (qi - kb0)) * BLK
    q_t = q_ref[0]
    scores = []
    for j in range(WIN_BLOCKS):
        start = pl.multiple_of(row0 + j * BLK, BLK)
        sc = jnp.dot(k_ref[0, 0, kb0 + j], q_t, preferred_element_type=F32)
        scores.append(sc + bm_ref[pl.ds(start, BLK), :])
    m = jnp.max(scores[0], axis=0, keepdims=True)
    for sc in scores[1:]:
        m = jnp.maximum(m, jnp.max(sc, axis=0, keepdims=True))
    denom = jnp.zeros((1, BLK), F32)
    out = jnp.zeros((HEAD_DIM, BLK), F32)
    for j in range(WIN_BLOCKS):
        p = jnp.exp(scores[j] - m)
        denom = denom + jnp.sum(p, axis=0, keepdims=True)
        out = out + jnp.dot(v_ref[0, 0, kb0 + j], p.astype(BF16), preferred_element_type=F32)
    o_ref[0] = (out / denom).astype(BF16)


def _chunk_call(q_t, k, v_t, rel_bias):
    b, _, s = q_t.shape
    nb = s // BLK
    lines = _bias_lines(rel_bias)
    return pl.pallas_call(
        _chunk_kernel,
        grid=(b, N_HEADS, nb),
        in_specs=[pl.BlockSpec((1, HEAD_DIM, BLK), lambda bi, h, qi: (bi, h, qi)),
                  pl.BlockSpec((1, 1, nb, BLK, HEAD_DIM), lambda bi, h, qi: (bi, h, 0, 0, 0)),
                  pl.BlockSpec((1, 1, nb, HEAD_DIM, BLK), lambda bi, h, qi: (bi, h, 0, 0, 0)),
                  pl.BlockSpec((1,) + lines.shape[1:], lambda bi, h, qi: (h, 0, 0))],
        out_specs=pl.BlockSpec((1, HEAD_DIM, BLK), lambda bi, h, qi: (bi, h, qi)),
        out_shape=jax.ShapeDtypeStruct((b, D_GRP, s), BF16),
        scratch_shapes=[pltpu.VMEM((TABLE_ROWS, BLK), F32)],
        compiler_params=pltpu.CompilerParams(
            dimension_semantics=("parallel", "parallel", "arbitrary"), vmem_limit_bytes=VMEM_LIMIT),
        name="chunk",
    )(q_t, k, v_t, lines)


def _bias_lines(rel_bias):
    m = np.arange(2 * BLK)[None, :]
    diff = np.where(m < BLK, m, m - 2 * BLK)
    rel = diff + ((WIN_BLOCKS - 1) - np.arange(TABLE_ROWS // BLK)[:, None]) * BLK
    return rel_bias[:, np.clip(rel, -REL_CLIP, REL_CLIP) + REL_CLIP].astype(F32)


def _layer_norm(v, g, b):
    mu = jnp.mean(v, axis=-1, keepdims=True)
    c = v - mu
    var = jnp.mean(c * c, axis=-1, keepdims=True)
    return c * lax.rsqrt(var + LN_EPS) * g + b


def _post_kernel(alpha, ya_ref, yc_ref, x_ref, woa_ref, woc_ref, g_ref, b_ref, wr_ref, br_ref,
                 h_ref, hb_ref, cw_ref):
    mix = _tn_dot(ya_ref[0], woa_ref[...]) + _tn_dot(yc_ref[0], woc_ref[...])
    h = _layer_norm(alpha * x_ref[0] + mix, g_ref[...], b_ref[...])
    h_ref[...] = h
    hb_ref[...] = h.astype(BF16)

    logits = jnp.dot(h, wr_ref[...], preferred_element_type=F32,
                     precision=lax.Precision.HIGHEST) + br_ref[...]
    lane = lax.broadcasted_iota(jnp.int32, logits.shape, 1)
    big = jnp.int32(LANES)
    is_g = (lane >= N_EXPERTS) & (lane < N_EXPERTS + N_GROUPS)
    gl = jnp.where(is_g, logits, NEG_BIG)
    gmax = jnp.max(gl, axis=-1, keepdims=True)
    g_gate = 1.0 / jnp.sum(jnp.exp(gl - gmax), axis=-1, keepdims=True)
    g_idx = jnp.min(jnp.where(gl == gmax, lane, big), axis=-1, keepdims=True) - N_EXPERTS
    in_grp = (lane >= g_idx * EXPERTS_PER_GROUP) & (lane < (g_idx + 1) * EXPERTS_PER_GROUP)
    el = jnp.where(in_grp, logits, NEG_BIG)
    e1 = jnp.max(el, axis=-1, keepdims=True)
    i1 = jnp.min(jnp.where(el == e1, lane, big), axis=-1, keepdims=True)
    el2 = jnp.where(lane == i1, NEG_BIG, el)
    e2 = jnp.max(el2, axis=-1, keepdims=True)
    i2 = jnp.min(jnp.where(el2 == e2, lane, big), axis=-1, keepdims=True)
    r = jnp.exp(e2 - e1)
    gate1 = g_gate / (1.0 + r)
    gate2 = gate1 * r
    cw_ref[...] = jnp.where(lane == i1, gate1,
                            jnp.where(lane == i2, gate2,
                                      jnp.where(lane == g_idx + N_EXPERTS, 1.0, 0.0)))


def _post_call(ya_t, yc_t, x, woa, woc, g, b, wr, br, alpha):
    bsz, s, d = x.shape
    nb = s // BLK
    n = bsz * s
    row = lambda bi, t: (bi * nb + t, 0)
    full = lambda a: pl.BlockSpec(a.shape, lambda bi, t: (0,) * a.ndim)
    return pl.pallas_call(
        functools.partial(_post_kernel, alpha),
        grid=(bsz, nb),
        in_specs=[pl.BlockSpec((1, D_GRP, BLK), lambda bi, t: (bi, 0, t)),
                  pl.BlockSpec((1, D_GRP, BLK), lambda bi, t: (bi, 0, t)),
                  pl.BlockSpec((1, BLK, d), lambda bi, t: (bi, t, 0)),
                  full(woa), full(woc), full(g), full(b), full(wr), full(br)],
        out_specs=[pl.BlockSpec((BLK, d), row), pl.BlockSpec((BLK, d), row),
                   pl.BlockSpec((BLK, LANES), row)],
        out_shape=[jax.ShapeDtypeStruct((n, d), F32), jax.ShapeDtypeStruct((n, d), BF16),
                   jax.ShapeDtypeStruct((n, LANES), F32)],
        compiler_params=pltpu.CompilerParams(
            dimension_semantics=("parallel", "parallel"), vmem_limit_bytes=VMEM_LIMIT),
        name="post",
    )(ya_t, yc_t, x, woa, woc, g, b, wr, br)


MOE_T = 1024
MOE_CH = 128
MOE_VMEM_LIMIT = 60000 * 1024


def _moe_kernel(alpha, hb_ref, h_ref, cw_ref, wg_ref, wu_ref, wd_ref, g_ref, b_ref, o_ref, tri_ref):
    ti = pl.program_id(0)
    g = pl.program_id(1)

    @pl.when((ti == 0) & (g == 0))
    def _():
        r = lax.broadcasted_iota(jnp.int32, (MOE_T, MOE_T), 0)
        c = lax.broadcasted_iota(jnp.int32, (MOE_T, MOE_T), 1)
        tri_ref[...] = (r < c).astype(BF16)

    @pl.when(g == 0)
    def _():
        o_ref[...] = jnp.zeros_like(o_ref)

    cwt = cw_ref[...]
    lane = lax.broadcasted_iota(jnp.int32, (MOE_T, LANES), 1)
    sr = lax.broadcasted_iota(jnp.int32, (LANES, LANES), 0)
    sl = lax.broadcasted_iota(jnp.int32, (LANES, LANES), 1)
    sel = ((sl == sr + N_EXPERTS) & (sr < N_GROUPS)).astype(BF16)
    member = _nt_dot(sel, cwt.astype(BF16))
    rank = jnp.dot(member.astype(BF16), tri_ref[...], preferred_element_type=F32)
    rowi = lax.broadcasted_iota(jnp.int32, (LANES, MOE_T), 0)
    in_row = jnp.sum(jnp.where(rowi == g, member, 0.0), axis=0, keepdims=True)
    rank_row = jnp.sum(jnp.where(rowi == g, rank, 0.0), axis=0, keepdims=True)
    rank_col = jnp.sum(jnp.where(lane == g, rank.T, 0.0), axis=1, keepdims=True)
    in_col = jnp.sum(jnp.where(lane == g + N_EXPERTS, cwt, 0.0), axis=1, keepdims=True)
    count = jnp.sum(in_row).astype(jnp.int32)
    c1 = cwt.astype(BF16)
    rem = cwt - c1.astype(F32)
    c2 = rem.astype(BF16)
    c3 = (rem - c2.astype(F32)).astype(BF16)
    hb = hb_ref[...]
    df = wg_ref.shape[2] // EXPERTS_PER_GROUP
    chunk_row = lax.broadcasted_iota(jnp.int32, (MOE_CH, MOE_T), 0).astype(F32)
    chunk_lane = lax.broadcasted_iota(jnp.int32, (MOE_T, MOE_CH), 1).astype(F32)
    lane_c = lax.broadcasted_iota(jnp.int32, (MOE_CH, LANES), 1)

    def body(c, carry):
        base = (c * MOE_CH).astype(F32)
        p = ((rank_row == base + chunk_row) & (in_row > 0.0)).astype(BF16)
        x = jnp.dot(p, hb, preferred_element_type=F32).astype(BF16)
        cwc = (jnp.dot(p, c1, preferred_element_type=F32)
               + jnp.dot(p, c2, preferred_element_type=F32)
               + jnp.dot(p, c3, preferred_element_type=F32))
        gate = jnp.dot(x, wg_ref[0], preferred_element_type=F32)
        up = jnp.dot(x, wu_ref[0], preferred_element_type=F32)
        hid = []
        for e in range(EXPERTS_PER_GROUP):
            col = jnp.sum(jnp.where(lane_c == g * EXPERTS_PER_GROUP + e, cwc, 0.0),
                          axis=1, keepdims=True)
            ge = gate[:, e * df:(e + 1) * df]
            hid.append((ge * (1.0 / (1.0 + jnp.exp(-ge))) * up[:, e * df:(e + 1) * df] * col)
                       .astype(BF16))
        out = jnp.dot(jnp.concatenate(hid, axis=1), wd_ref[0], preferred_element_type=F32)
        p_t = ((rank_col == base + chunk_lane) & (in_col > 0.0)).astype(BF16)
        o_ref[...] += jnp.dot(p_t, out.astype(BF16), preferred_element_type=F32)
        return carry

    lax.fori_loop(0, (count + MOE_CH - 1) // MOE_CH, body, 0)

    @pl.when(g == pl.num_programs(1) - 1)
    def _():
        o_ref[...] = _layer_norm(alpha * h_ref[...] + o_ref[...], g_ref[...], b_ref[...])


def _moe_call(hb, h, cw, wg, wu, wd, g, b, alpha):
    n, d = h.shape
    ng, _, dh = wg.shape
    tok = lambda t, q: (t, 0)
    full = lambda a: pl.BlockSpec(a.shape, lambda t, q: (0,) * a.ndim)
    return pl.pallas_call(
        functools.partial(_moe_kernel, alpha),
        grid=(n // MOE_T, ng),
        in_specs=[pl.BlockSpec((MOE_T, d), tok), pl.BlockSpec((MOE_T, d), tok),
                  pl.BlockSpec((MOE_T, LANES), tok),
                  pl.BlockSpec((1, d, dh), lambda t, q: (q, 0, 0)),
                  pl.BlockSpec((1, d, dh), lambda t, q: (q, 0, 0)),
                  pl.BlockSpec((1, dh, d), lambda t, q: (q, 0, 0)),
                  full(g), full(b)],
        out_specs=pl.BlockSpec((MOE_T, d), tok),
        out_shape=jax.ShapeDtypeStruct((n, d), F32),
        scratch_shapes=[pltpu.VMEM((MOE_T, MOE_T), BF16)],
        compiler_params=pltpu.CompilerParams(
            dimension_semantics=("arbitrary", "arbitrary"), vmem_limit_bytes=MOE_VMEM_LIMIT),
        name="moe",
    )(hb, h, cw, wg, wu, wd, g, b)


def _layer(x, w_in, w_out, rel_bias, ln1_g, ln1_b, w_group, b_group, w_expert, b_expert,
           w_gate, w_up, w_down, ln2_g, ln2_b, alpha):
    bsz, s, d = x.shape
    scale = HEAD_DIM ** -0.5
    wqa, wka, wva, wqc, wkc, wvc = [w_in[:, i * D_GRP:(i + 1) * D_GRP] for i in range(6)]
    wnat = jnp.concatenate([wqa * scale, wqc * scale, wvc], axis=1).T.astype(BF16)
    qa_t, qc_t, vc_t, va_t, ka, kc = _proj_call(
        x, wnat, wva.T.astype(BF16), wka.astype(BF16), wkc.astype(BF16))

    ya_t = _stick_call(qa_t, ka, va_t)
    yc_t = _chunk_call(qc_t, kc, vc_t, rel_bias)

    wo = w_out.astype(BF16)
    wr = jnp.zeros((d, LANES), F32)
    wr = wr.at[:, :N_EXPERTS].set(w_expert.transpose(1, 0, 2).reshape(d, N_EXPERTS))
    wr = wr.at[:, N_EXPERTS:N_EXPERTS + N_GROUPS].set(w_group)
    br = jnp.zeros((1, LANES), F32)
    br = br.at[0, :N_EXPERTS].set(b_expert.reshape(N_EXPERTS))
    br = br.at[0, N_EXPERTS:N_EXPERTS + N_GROUPS].set(b_group)
    h, hb, cw = _post_call(ya_t, yc_t, x, wo[:D_GRP], wo[D_GRP:], ln1_g[None], ln1_b[None],
                           wr, br, alpha)

    df = w_gate.shape[-1]
    by_group = lambda w: w.astype(BF16).transpose(0, 2, 1, 3).reshape(
        N_GROUPS, d, EXPERTS_PER_GROUP * df)
    out = _moe_call(hb, h, cw, by_group(w_gate), by_group(w_up),
                    w_down.astype(BF16).reshape(N_GROUPS, EXPERTS_PER_GROUP * df, d),
                    ln2_g[None], ln2_b[None], alpha)
    return out.reshape(bsz, s, d)


def kernel(x, w_in, w_out, rel_bias, ln1_g, ln1_b, w_group, b_group, w_expert, b_expert,
           w_gate, w_up, w_down, ln2_g, ln2_b):
    depth = w_in.shape[0]
    alpha = (2.0 * depth) ** 0.25
    for l in range(depth):
        x = _layer(x, w_in[l], w_out[l], rel_bias[l], ln1_g[l], ln1_b[l], w_group[l], b_group[l],
                   w_expert[l], b_expert[l], w_gate[l], w_up[l], w_down[l], ln2_g[l], ln2_b[l], alpha)
    return x
```

```python
import functools

import numpy as np
import jax
import jax.numpy as jnp
from jax import lax
from jax.experimental import pallas as pl
from jax.experimental.pallas import tpu as pltpu

F32 = jnp.float32
BF16 = jnp.bfloat16

HEAD_DIM = 64
N_HEADS = 8
D_GRP = N_HEADS * HEAD_DIM
CHUNK = 64
LEFT_CHUNKS = 8
REL_CLIP = 128
N_GROUPS = 4
EXPERTS_PER_GROUP = 8
N_EXPERTS = N_GROUPS * EXPERTS_PER_GROUP
LN_EPS = 1e-5
NEG_BIG = -1e30
LOG2E = float(np.log2(np.e))

SUBLANES = 8
LANES = 128
BLK = 256
ROWGROUPS = BLK // SUBLANES
VMEM_LIMIT = 56 * 1024 * 1024


def _nt_dot(a, b):
    return lax.dot_general(a, b, (((1,), (1,)), ((), ())), preferred_element_type=F32)


def _tn_dot(a, b):
    return lax.dot_general(a, b, (((0,), (0,)), ((), ())), preferred_element_type=F32)


def _proj_kernel(x_ref, perm_ref, wnat_ref, wva_ref, wka_ref, wkc_ref,
                 qa_ref, qc_ref, vc_ref, va_ref, ka_ref, kc_ref):
    xb = x_ref[0].astype(BF16)
    xp = jnp.dot(perm_ref[...], xb, preferred_element_type=F32).astype(BF16)
    nat = _nt_dot(wnat_ref[...], xb)
    qa_ref[0] = nat[0:D_GRP].astype(BF16)
    qc_ref[0] = nat[D_GRP:2 * D_GRP].astype(BF16)
    vc = nat[2 * D_GRP:3 * D_GRP].astype(BF16)
    va = _nt_dot(wva_ref[...], xp).astype(BF16)
    ka = jnp.dot(xp, wka_ref[...], preferred_element_type=F32).astype(BF16)
    kc = jnp.dot(xb, wkc_ref[...], preferred_element_type=F32).astype(BF16)
    for h in range(N_HEADS):
        sl = slice(h * HEAD_DIM, (h + 1) * HEAD_DIM)
        vc_ref[0, h, 0] = vc[sl]
        va_ref[0, h, 0] = va[sl]
        ka_ref[0, h, 0] = ka[:, sl]
        kc_ref[0, h, 0] = kc[:, sl]


def _key_permutation():
    r = np.arange(BLK)
    src = (r % SUBLANES) * ROWGROUPS + (ROWGROUPS - 1 - r // SUBLANES)
    return jnp.asarray(np.eye(BLK, dtype=np.float32)[src], dtype=BF16)


def _proj_call(x, wnat, wva, wka, wkc):
    b, s, d = x.shape
    nb = s // BLK
    perm = _key_permutation()
    qspec = pl.BlockSpec((1, D_GRP, BLK), lambda bi, t: (bi, 0, t))
    vspec = pl.BlockSpec((1, N_HEADS, 1, HEAD_DIM, BLK), lambda bi, t: (bi, 0, t, 0, 0))
    kspec = pl.BlockSpec((1, N_HEADS, 1, BLK, HEAD_DIM), lambda bi, t: (bi, 0, t, 0, 0))
    qshape = jax.ShapeDtypeStruct((b, D_GRP, s), BF16)
    vshape = jax.ShapeDtypeStruct((b, N_HEADS, nb, HEAD_DIM, BLK), BF16)
    kshape = jax.ShapeDtypeStruct((b, N_HEADS, nb, BLK, HEAD_DIM), BF16)
    full = lambda a: pl.BlockSpec(a.shape, lambda bi, t: (0,) * a.ndim)
    return pl.pallas_call(
        _proj_kernel,
        grid=(b, nb),
        in_specs=[pl.BlockSpec((1, BLK, d), lambda bi, t: (bi, t, 0)),
                  full(perm), full(wnat), full(wva), full(wka), full(wkc)],
        out_specs=[qspec, qspec, vspec, vspec, kspec, kspec],
        out_shape=[qshape, qshape, vshape, vshape, kshape, kshape],
        compiler_params=pltpu.CompilerParams(
            dimension_semantics=("parallel", "parallel"), vmem_limit_bytes=VMEM_LIMIT),
        name="proj",
    )(x, perm, wnat, wva, wka, wkc)


def _sublane_shift_up(t, d, sub):
    rolled = pltpu.roll(t, SUBLANES - d, axis=0)
    return jnp.where(sub < SUBLANES - d, rolled, 0.0)


def _stick_scan(z_ref, u_ref, carry, diag_cols, w_ref=None, prev_offset=None):
    n = z_ref.shape[1]
    sub = lax.broadcasted_iota(jnp.int32, (SUBLANES, n), 0)
    run = jnp.zeros((SUBLANES, n), F32)
    prev_w = None
    for i in range(ROWGROUPS):
        rows = slice(i * SUBLANES, (i + 1) * SUBLANES)
        if w_ref is not None:
            w = jnp.exp2(u_ref[rows, :] - prev_offset)
            if i % 2 == 0:
                prev_w = w
            else:
                w_ref[(i - 1) * SUBLANES:(i + 1) * SUBLANES, :] = jnp.concatenate(
                    [prev_w, w], axis=0).astype(BF16)
        zi = z_ref[rows, :]
        sp = jnp.maximum(zi, 0.0) + jnp.log2(1.0 + jnp.exp2(-jnp.abs(zi)))
        if diag_cols is None:
            run = run + sp
            u_ref[rows, :] = zi - run
        else:
            causal = diag_cols > -i
            run = run + jnp.where(causal, sp, 0.0)
            u_ref[rows, :] = jnp.where(causal, zi - run, NEG_BIG)
    s1 = run + _sublane_shift_up(run, 1, sub)
    s2 = s1 + _sublane_shift_up(s1, 2, sub)
    s3 = s2 + _sublane_shift_up(s2, 4, sub)
    offset = (s3 - run) + carry
    new_carry = jnp.broadcast_to(s3[0:1, :], (SUBLANES, n)) + carry
    return offset, new_carry


def _stick_exp(u_ref, offset):
    rows = [jnp.exp2(u_ref[i * SUBLANES:(i + 1) * SUBLANES, :] - offset) for i in range(ROWGROUPS)]
    return jnp.concatenate(rows, axis=0).astype(BF16)


def _stick_kernel(q_ref, k_ref, v_ref, o_ref, z_scr, u_scr, w_scr):
    qi = pl.program_id(2)
    last = k_ref.shape[2] - 1
    q_t = q_ref[0]
    sub = lax.broadcasted_iota(jnp.int32, (SUBLANES, BLK), 0)
    col = lax.broadcasted_iota(jnp.int32, (SUBLANES, BLK), 1)
    diag_cols = col - ROWGROUPS * sub - (ROWGROUPS - 1)

    def logits(kb):
        return jnp.dot(k_ref[0, 0, kb], q_t, preferred_element_type=F32) * LOG2E

    z_scr[...] = logits(qi)
    w_scr[...] = jnp.zeros_like(w_scr)
    offset, carry = _stick_scan(z_scr, u_scr, jnp.zeros((SUBLANES, BLK), F32), diag_cols)
    z_scr[...] = logits(jnp.maximum(qi - 1, 0))

    def body(t, state):
        acc, offset, carry = state
        kb = qi - 1 - t
        pv = jnp.dot(v_ref[0, 0, jnp.minimum(kb + 2, last)], w_scr[...], preferred_element_type=F32)
        offset, carry = _stick_scan(z_scr, u_scr, carry, None, w_scr, offset)
        z_scr[...] = logits(jnp.maximum(kb - 1, 0))
        return acc + pv, offset, carry

    acc, offset, _ = lax.fori_loop(0, qi, body, (jnp.zeros((HEAD_DIM, BLK), F32), offset, carry))
    acc = acc + jnp.dot(v_ref[0, 0, min(1, last)], w_scr[...], preferred_element_type=F32)
    acc = acc + jnp.dot(v_ref[0, 0, 0], _stick_exp(u_scr, offset), preferred_element_type=F32)
    o_ref[0] = acc.astype(BF16)


def _stick_call(q_t, k, v_t):
    b, _, s = q_t.shape
    nb = s // BLK
    return pl.pallas_call(
        _stick_kernel,
        grid=(b, N_HEADS, nb),
        in_specs=[pl.BlockSpec((1, HEAD_DIM, BLK), lambda bi, h, qi: (bi, h, qi)),
                  pl.BlockSpec((1, 1, nb, BLK, HEAD_DIM), lambda bi, h, qi: (bi, h, 0, 0, 0)),
                  pl.BlockSpec((1, 1, nb, HEAD_DIM, BLK), lambda bi, h, qi: (bi, h, 0, 0, 0))],
        out_specs=pl.BlockSpec((1, HEAD_DIM, BLK), lambda bi, h, qi: (bi, h, qi)),
        out_shape=jax.ShapeDtypeStruct((b, D_GRP, s), BF16),
        scratch_shapes=[pltpu.VMEM((BLK, BLK), F32), pltpu.VMEM((BLK, BLK), F32),
                        pltpu.VMEM((BLK, BLK), BF16)],
        compiler_params=pltpu.CompilerParams(
            dimension_semantics=("parallel", "parallel", "arbitrary"), vmem_limit_bytes=VMEM_LIMIT),
        name="stick",
    )(q_t, k, v_t)


WIN_BLOCKS = LEFT_CHUNKS * CHUNK // BLK + 1
TABLE_ROWS = (2 * WIN_BLOCKS - 1) * BLK


def _chunk_kernel(q_ref, k_ref, v_ref, lines_ref, o_ref, bm_ref):
    qi = pl.program_id(2)

    @pl.when(qi == 0)
    def _():
        row = lax.broadcasted_iota(jnp.int32, (BLK, BLK), 0)
        qchunk = lax.broadcasted_iota(jnp.int32, (BLK, BLK), 1) // CHUNK
        for rb in range(TABLE_ROWS // BLK):
            line = jnp.broadcast_to(lines_ref[0, rb:rb + 1, :], (BLK, 2 * BLK))
            bias = pltpu.roll(line, 0, 1, stride=1, stride_axis=0)[:, :BLK]
            kchunk = (row + (rb - (WIN_BLOCKS - 1)) * BLK) >> 6
            dchunk = qchunk - kchunk
            visible = (dchunk >= 0) & (dchunk <= LEFT_CHUNKS)
            bm_ref[rb * BLK:(rb + 1) * BLK, :] = jnp.where(visible, bias, NEG_BIG)

    kb0 = jnp.maximum(qi - (WIN_BLOCKS - 1), 0)
    row0 = (WIN_BLOCKS - 1 - (qi - kb0)) * BLK
    q_t = q_ref[0]
    scores = []
    for j in range(WIN_BLOCKS):
        start = pl.multiple_of(row0 + j * BLK, BLK)
        sc = jnp.dot(k_ref[0, 0, kb0 + j], q_t, preferred_element_type=F32)
        scores.append(sc + bm_ref[pl.ds(start, BLK), :])
    m = jnp.max(scores[0], axis=0, keepdims=True)
    for sc in scores[1:]:
        m = jnp.maximum(m, jnp.max(sc, axis=0, keepdims=True))
    denom = jnp.zeros((1, BLK), F32)
    out = jnp.zeros((HEAD_DIM, BLK), F32)
    for j in range(WIN_BLOCKS):
        p = jnp.exp(scores[j] - m)
        denom = denom + jnp.sum(p, axis=0, keepdims=True)
        out = out + jnp.dot(v_ref[0, 0, kb0 + j], p.astype(BF16), preferred_element_type=F32)
    o_ref[0] = (out / denom).astype(BF16)


def _chunk_call(q_t, k, v_t, rel_bias):
    b, _, s = q_t.shape
    nb = s // BLK
    lines = _bias_lines(rel_bias)
    return pl.pallas_call(
        _chunk_kernel,
        grid=(b, N_HEADS, nb),
        in_specs=[pl.BlockSpec((1, HEAD_DIM, BLK), lambda bi, h, qi: (bi, h, qi)),
                  pl.BlockSpec((1, 1, nb, BLK, HEAD_DIM), lambda bi, h, qi: (bi, h, 0, 0, 0)),
                  pl.BlockSpec((1, 1, nb, HEAD_DIM, BLK), lambda bi, h, qi: (bi, h, 0, 0, 0)),
                  pl.BlockSpec((1,) + lines.shape[1:], lambda bi, h, qi: (h, 0, 0))],
        out_specs=pl.BlockSpec((1, HEAD_DIM, BLK), lambda bi, h, qi: (bi, h, qi)),
        out_shape=jax.ShapeDtypeStruct((b, D_GRP, s), BF16),
        scratch_shapes=[pltpu.VMEM((TABLE_ROWS, BLK), F32)],
        compiler_params=pltpu.CompilerParams(
            dimension_semantics=("parallel", "parallel", "arbitrary"), vmem_limit_bytes=VMEM_LIMIT),
        name="chunk",
    )(q_t, k, v_t, lines)


def _bias_lines(rel_bias):
    m = np.arange(2 * BLK)[None, :]
    diff = np.where(m < BLK, m, m - 2 * BLK)
    rel = diff + ((WIN_BLOCKS - 1) - np.arange(TABLE_ROWS // BLK)[:, None]) * BLK
    return rel_bias[:, np.clip(rel, -REL_CLIP, REL_CLIP) + REL_CLIP].astype(F32)


def _layer_norm(v, g, b):
    mu = jnp.mean(v, axis=-1, keepdims=True)
    c = v - mu
    var = jnp.mean(c * c, axis=-1, keepdims=True)
    return c * lax.rsqrt(var + LN_EPS) * g + b


def _post_kernel(alpha, ya_ref, yc_ref, x_ref, woa_ref, woc_ref, g_ref, b_ref, wr_ref, br_ref,
                 h_ref, hb_ref, cw_ref):
    mix = _tn_dot(ya_ref[0], woa_ref[...]) + _tn_dot(yc_ref[0], woc_ref[...])
    h = _layer_norm(alpha * x_ref[0] + mix, g_ref[...], b_ref[...])
    h_ref[...] = h
    hb_ref[...] = h.astype(BF16)

    logits = jnp.dot(h, wr_ref[...], preferred_element_type=F32,
                     precision=lax.Precision.HIGHEST) + br_ref[...]
    lane = lax.broadcasted_iota(jnp.int32, logits.shape, 1)
    big = jnp.int32(LANES)
    is_g = (lane >= N_EXPERTS) & (lane < N_EXPERTS + N_GROUPS)
    gl = jnp.where(is_g, logits, NEG_BIG)
    gmax = jnp.max(gl, axis=-1, keepdims=True)
    g_gate = 1.0 / jnp.sum(jnp.exp(gl - gmax), axis=-1, keepdims=True)
    g_idx = jnp.min(jnp.where(gl == gmax, lane, big), axis=-1, keepdims=True) - N_EXPERTS
    in_grp = (lane >= g_idx * EXPERTS_PER_GROUP) & (lane < (g_idx + 1) * EXPERTS_PER_GROUP)
    el = jnp.where(in_grp, logits, NEG_BIG)
    e1 = jnp.max(el, axis=-1, keepdims=True)
    i1 = jnp.min(jnp.where(el == e1, lane, big), axis=-1, keepdims=True)
    el2 = jnp.where(lane == i1, NEG_BIG, el)
    e2 = jnp.max(el2, axis=-1, keepdims=True)
    i2 = jnp.min(jnp.where(el2 == e2, lane, big), axis=-1, keepdims=True)
    r = jnp.exp(e2 - e1)
    gate1 = g_gate / (1.0 + r)
    gate2 = gate1 * r
    cw_ref[...] = jnp.where(lane == i1, gate1,
                            jnp.where(lane == i2, gate2,
                                      jnp.where(lane == g_idx + N_EXPERTS, 1.0, 0.0)))


def _post_call(ya_t, yc_t, x, woa, woc, g, b, wr, br, alpha):
    bsz, s, d = x.shape
    nb = s // BLK
    n = bsz * s
    row = lambda bi, t: (bi * nb + t, 0)
    full = lambda a: pl.BlockSpec(a.shape, lambda bi, t: (0,) * a.ndim)
    return pl.pallas_call(
        functools.partial(_post_kernel, alpha),
        grid=(bsz, nb),
        in_specs=[pl.BlockSpec((1, D_GRP, BLK), lambda bi, t: (bi, 0, t)),
                  pl.BlockSpec((1, D_GRP, BLK), lambda bi, t: (bi, 0, t)),
                  pl.BlockSpec((1, BLK, d), lambda bi, t: (bi, t, 0)),
                  full(woa), full(woc), full(g), full(b), full(wr), full(br)],
        out_specs=[pl.BlockSpec((BLK, d), row), pl.BlockSpec((BLK, d), row),
                   pl.BlockSpec((BLK, LANES), row)],
        out_shape=[jax.ShapeDtypeStruct((n, d), F32), jax.ShapeDtypeStruct((n, d), BF16),
                   jax.ShapeDtypeStruct((n, LANES), F32)],
        compiler_params=pltpu.CompilerParams(
            dimension_semantics=("parallel", "parallel"), vmem_limit_bytes=VMEM_LIMIT),
        name="post",
    )(ya_t, yc_t, x, woa, woc, g, b, wr, br)


MOE_T = 1024
MOE_CH = 128
MOE_VMEM_LIMIT = 60000 * 1024


def _moe_kernel(alpha, hb_ref, h_ref, cw_ref, wg_ref, wu_ref, wd_ref, g_ref, b_ref, o_ref, tri_ref):
    ti = pl.program_id(0)
    g = pl.program_id(1)

    @pl.when((ti == 0) & (g == 0))
    def _():
        r = lax.broadcasted_iota(jnp.int32, (MOE_T, MOE_T), 0)
        c = lax.broadcasted_iota(jnp.int32, (MOE_T, MOE_T), 1)
        tri_ref[...] = (r < c).astype(BF16)

    @pl.when(g == 0)
    def _():
        o_ref[...] = jnp.zeros_like(o_ref)

    cwt = cw_ref[...]
    lane = lax.broadcasted_iota(jnp.int32, (MOE_T, LANES), 1)
    sr = lax.broadcasted_iota(jnp.int32, (LANES, LANES), 0)
    sl = lax.broadcasted_iota(jnp.int32, (LANES, LANES), 1)
    sel = ((sl == sr + N_EXPERTS) & (sr < N_GROUPS)).astype(BF16)
    member = _nt_dot(sel, cwt.astype(BF16))
    rank = jnp.dot(member.astype(BF16), tri_ref[...], preferred_element_type=F32)
    rowi = lax.broadcasted_iota(jnp.int32, (LANES, MOE_T), 0)
    in_row = jnp.sum(jnp.where(rowi == g, member, 0.0), axis=0, keepdims=True)
    rank_row = jnp.sum(jnp.where(rowi == g, rank, 0.0), axis=0, keepdims=True)
    rank_col = jnp.sum(jnp.where(lane == g, rank.T, 0.0), axis=1, keepdims=True)
    in_col = jnp.sum(jnp.where(lane == g + N_EXPERTS, cwt, 0.0), axis=1, keepdims=True)
    count = jnp.sum(in_row).astype(jnp.int32)
    c1 = cwt.astype(BF16)
    rem = cwt - c1.astype(F32)
    c2 = rem.astype(BF16)
    c3 = (rem - c2.astype(F32)).astype(BF16)
    hb = hb_ref[...]
    df = wg_ref.shape[2] // EXPERTS_PER_GROUP
    chunk_row = lax.broadcasted_iota(jnp.int32, (MOE_CH, MOE_T), 0).astype(F32)
    chunk_lane = lax.broadcasted_iota(jnp.int32, (MOE_T, MOE_CH), 1).astype(F32)
    lane_c = lax.broadcasted_iota(jnp.int32, (MOE_CH, LANES), 1)

    def body(c, carry):
        base = (c * MOE_CH).astype(F32)
        p = ((rank_row == base + chunk_row) & (in_row > 0.0)).astype(BF16)
        x = jnp.dot(p, hb, preferred_element_type=F32).astype(BF16)
        cwc = (jnp.dot(p, c1, preferred_element_type=F32)
               + jnp.dot(p, c2, preferred_element_type=F32)
               + jnp.dot(p, c3, preferred_element_type=F32))
        gate = jnp.dot(x, wg_ref[0], preferred_element_type=F32)
        up = jnp.dot(x, wu_ref[0], preferred_element_type=F32)
        hid = []
        for e in range(EXPERTS_PER_GROUP):
            col = jnp.sum(jnp.where(lane_c == g * EXPERTS_PER_GROUP + e, cwc, 0.0),
                          axis=1, keepdims=True)
            ge = gate[:, e * df:(e + 1) * df]
            hid.append((ge * (1.0 / (1.0 + jnp.exp(-ge))) * up[:, e * df:(e + 1) * df] * col)
                       .astype(BF16))
        out = jnp.dot(jnp.concatenate(hid, axis=1), wd_ref[0], preferred_element_type=F32)
        p_t = ((rank_col == base + chunk_lane) & (in_col > 0.0)).astype(BF16)
        o_ref[...] += jnp.dot(p_t, out.astype(BF16), preferred_element_type=F32)
        return carry

    lax.fori_loop(0, (count + MOE_CH - 1) // MOE_CH, body, 0)

    @pl.when(g == pl.num_programs(1) - 1)
    def _():
        o_ref[...] = _layer_norm(alpha * h_ref[...] + o_ref[...], g_ref[...], b_ref[...])


def _moe_call(hb, h, cw, wg, wu, wd, g, b, alpha):
    n, d = h.shape
    ng, _, dh = wg.shape
    tok = lambda t, q: (t, 0)
    full = lambda a: pl.BlockSpec(a.shape, lambda t, q: (0,) * a.ndim)
    return pl.pallas_call(
        functools.partial(_moe_kernel, alpha),
        grid=(n // MOE_T, ng),
        in_specs=[pl.BlockSpec((MOE_T, d), tok), pl.BlockSpec((MOE_T, d), tok),
                  pl.BlockSpec((MOE_T, LANES), tok),
                  pl.BlockSpec((1, d, dh), lambda t, q: (q, 0, 0)),
                  pl.BlockSpec((1, d, dh), lambda t, q: (q, 0, 0)),
                  pl.BlockSpec((1, dh, d), lambda t, q: (q, 0, 0)),
                  full(g), full(b)],
        out_specs=pl.BlockSpec((MOE_T, d), tok),
        out_shape=jax.ShapeDtypeStruct((n, d), F32),
        scratch_shapes=[pltpu.VMEM((MOE_T, MOE_T), BF16)],
        compiler_params=pltpu.CompilerParams(
            dimension_semantics=("arbitrary", "arbitrary"), vmem_limit_bytes=MOE_VMEM_LIMIT),
        name="moe",
    )(hb, h, cw, wg, wu, wd, g, b)


def _layer(x, w_in, w_out, rel_bias, ln1_g, ln1_b, w_group, b_group, w_expert, b_expert,
           w_gate, w_up, w_down, ln2_g, ln2_b, alpha):
    bsz, s, d = x.shape
    scale = HEAD_DIM ** -0.5
    wqa, wka, wva, wqc, wkc, wvc = [w_in[:, i * D_GRP:(i + 1) * D_GRP] for i in range(6)]
    wnat = jnp.concatenate([wqa * scale, wqc * scale, wvc], axis=1).T.astype(BF16)
    qa_t, qc_t, vc_t, va_t, ka, kc = _proj_call(
        x, wnat, wva.T.astype(BF16), wka.astype(BF16), wkc.astype(BF16))

    ya_t = _stick_call(qa_t, ka, va_t)
    yc_t = _chunk_call(qc_t, kc, vc_t, rel_bias)

    wo = w_out.astype(BF16)
    wr = jnp.zeros((d, LANES), F32)
    wr = wr.at[:, :N_EXPERTS].set(w_expert.transpose(1, 0, 2).reshape(d, N_EXPERTS))
    wr = wr.at[:, N_EXPERTS:N_EXPERTS + N_GROUPS].set(w_group)
    br = jnp.zeros((1, LANES), F32)
    br = br.at[0, :N_EXPERTS].set(b_expert.reshape(N_EXPERTS))
    br = br.at[0, N_EXPERTS:N_EXPERTS + N_GROUPS].set(b_group)
    h, hb, cw = _post_call(ya_t, yc_t, x, wo[:D_GRP], wo[D_GRP:], ln1_g[None], ln1_b[None],
                           wr, br, alpha)

    df = w_gate.shape[-1]
    by_group = lambda w: w.astype(BF16).transpose(0, 2, 1, 3).reshape(
        N_GROUPS, d, EXPERTS_PER_GROUP * df)
    out = _moe_call(hb, h, cw, by_group(w_gate), by_group(w_up),
                    w_down.astype(BF16).reshape(N_GROUPS, EXPERTS_PER_GROUP * df, d),
                    ln2_g[None], ln2_b[None], alpha)
    return out.reshape(bsz, s, d)


def kernel(x, w_in, w_out, rel_bias, ln1_g, ln1_b, w_group, b_group, w_expert, b_expert,
           w_gate, w_up, w_down, ln2_g, ln2_b):
    depth = w_in.shape[0]
    alpha = (2.0 * depth) ** 0.25
    for l in range(depth):
        x = _layer(x, w_in[l], w_out[l], rel_bias[l], ln1_g[l], ln1_b[l], w_group[l], b_group[l],
                   w_expert[l], b_expert[l], w_gate[l], w_up[l], w_down[l], ln2_g[l], ln2_b[l], alpha)
    return x
```

```python
import functools

import numpy as np
import jax
import jax.numpy as jnp
from jax import lax
from jax.experimental import pallas as pl
from jax.experimental.pallas import tpu as pltpu

F32 = jnp.float32
BF16 = jnp.bfloat16

HEAD_DIM = 64
N_HEADS = 8
D_GRP = N_HEADS * HEAD_DIM
CHUNK = 64
LEFT_CHUNKS = 8
REL_CLIP = 128
N_GROUPS = 4
EXPERTS_PER_GROUP = 8
N_EXPERTS = N_GROUPS * EXPERTS_PER_GROUP
ROUTER_ROWS = 40
LN_EPS = 1e-5
NEG_BIG = -1e30
LOG2E = float(np.log2(np.e))

SUBLANES = 8
LANES = 128
BLK = 256
ROWGROUPS = BLK // SUBLANES
VMEM_LIMIT = 56 * 1024 * 1024


def _nt_dot(a, b):
    return lax.dot_general(a, b, (((1,), (1,)), ((), ())), preferred_element_type=F32)


def _tn_dot(a, b):
    return lax.dot_general(a, b, (((0,), (0,)), ((), ())), preferred_element_type=F32)


def _proj_kernel(x_ref, perm_ref, wnat_ref, wva_ref, wka_ref, wkc_ref,
                 qa_ref, qc_ref, vc_ref, va_ref, ka_ref, kc_ref):
    xb = x_ref[0].astype(BF16)
    xp = jnp.dot(perm_ref[...], xb, preferred_element_type=F32).astype(BF16)
    nat = _nt_dot(wnat_ref[...], xb)
    qa_ref[0] = nat[0:D_GRP].astype(BF16)
    qc_ref[0] = nat[D_GRP:2 * D_GRP].astype(BF16)
    vc = nat[2 * D_GRP:3 * D_GRP].astype(BF16)
    va = _nt_dot(wva_ref[...], xp).astype(BF16)
    ka = jnp.dot(xp, wka_ref[...], preferred_element_type=F32).astype(BF16)
    kc = jnp.dot(xb, wkc_ref[...], preferred_element_type=F32).astype(BF16)
    for h in range(N_HEADS):
        sl = slice(h * HEAD_DIM, (h + 1) * HEAD_DIM)
        vc_ref[0, h, 0] = vc[sl]
        va_ref[0, h, 0] = va[sl]
        ka_ref[0, h, 0] = ka[:, sl]
        kc_ref[0, h, 0] = kc[:, sl]


def _key_permutation():
    r = np.arange(BLK)
    src = (r % SUBLANES) * ROWGROUPS + (ROWGROUPS - 1 - r // SUBLANES)
    return jnp.asarray(np.eye(BLK, dtype=np.float32)[src], dtype=BF16)


def _proj_call(x, wnat, wva, wka, wkc):
    b, s, d = x.shape
    nb = s // BLK
    perm = _key_permutation()
    qspec = pl.BlockSpec((1, D_GRP, BLK), lambda bi, t: (bi, 0, t))
    vspec = pl.BlockSpec((1, N_HEADS, 1, HEAD_DIM, BLK), lambda bi, t: (bi, 0, t, 0, 0))
    kspec = pl.BlockSpec((1, N_HEADS, 1, BLK, HEAD_DIM), lambda bi, t: (bi, 0, t, 0, 0))
    qshape = jax.ShapeDtypeStruct((b, D_GRP, s), BF16)
    vshape = jax.ShapeDtypeStruct((b, N_HEADS, nb, HEAD_DIM, BLK), BF16)
    kshape = jax.ShapeDtypeStruct((b, N_HEADS, nb, BLK, HEAD_DIM), BF16)
    full = lambda a: pl.BlockSpec(a.shape, lambda bi, t: (0,) * a.ndim)
    return pl.pallas_call(
        _proj_kernel,
        grid=(b, nb),
        in_specs=[pl.BlockSpec((1, BLK, d), lambda bi, t: (bi, t, 0)),
                  full(perm), full(wnat), full(wva), full(wka), full(wkc)],
        out_specs=[qspec, qspec, vspec, vspec, kspec, kspec],
        out_shape=[qshape, qshape, vshape, vshape, kshape, kshape],
        compiler_params=pltpu.CompilerParams(
            dimension_semantics=("parallel", "parallel"), vmem_limit_bytes=VMEM_LIMIT),
        name="proj",
    )(x, perm, wnat, wva, wka, wkc)


def _sublane_shift_up(t, d, sub):
    rolled = pltpu.roll(t, SUBLANES - d, axis=0)
    return jnp.where(sub < SUBLANES - d, rolled, 0.0)


def _stick_scan(z_ref, u_ref, carry, diag_cols, w_ref=None, prev_offset=None):
    n = z_ref.shape[1]
    sub = lax.broadcasted_iota(jnp.int32, (SUBLANES, n), 0)
    run = jnp.zeros((SUBLANES, n), F32)
    prev_w = None
    for i in range(ROWGROUPS):
        rows = slice(i * SUBLANES, (i + 1) * SUBLANES)
        if w_ref is not None:
            w = jnp.exp2(u_ref[rows, :] - prev_offset)
            if i % 2 == 0:
                prev_w = w
            else:
                w_ref[(i - 1) * SUBLANES:(i + 1) * SUBLANES, :] = jnp.concatenate(
                    [prev_w, w], axis=0).astype(BF16)
        zi = z_ref[rows, :]
        sp = jnp.maximum(zi, 0.0) + jnp.log2(1.0 + jnp.exp2(-jnp.abs(zi)))
        if diag_cols is None:
            run = run + sp
            u_ref[rows, :] = zi - run
        else:
            causal = diag_cols > -i
            run = run + jnp.where(causal, sp, 0.0)
            u_ref[rows, :] = jnp.where(causal, zi - run, NEG_BIG)
    s1 = run + _sublane_shift_up(run, 1, sub)
    s2 = s1 + _sublane_shift_up(s1, 2, sub)
    s3 = s2 + _sublane_shift_up(s2, 4, sub)
    offset = (s3 - run) + carry
    new_carry = jnp.broadcast_to(s3[0:1, :], (SUBLANES, n)) + carry
    return offset, new_carry


def _stick_exp(u_ref, offset):
    rows = [jnp.exp2(u_ref[i * SUBLANES:(i + 1) * SUBLANES, :] - offset) for i in range(ROWGROUPS)]
    return jnp.concatenate(rows, axis=0).astype(BF16)


WIN_BLOCKS = LEFT_CHUNKS * CHUNK // BLK + 1
TABLE_ROWS = (2 * WIN_BLOCKS - 1) * BLK


def _build_bias_table(lines_ref, bm_ref):
    row = lax.broadcasted_iota(jnp.int32, (BLK, BLK), 0)
    qchunk = lax.broadcasted_iota(jnp.int32, (BLK, BLK), 1) // CHUNK
    for rb in range(TABLE_ROWS // BLK):
        line = jnp.broadcast_to(lines_ref[0, rb:rb + 1, :], (BLK, 2 * BLK))
        bias = pltpu.roll(line, 0, 1, stride=1, stride_axis=0)[:, :BLK]
        kchunk = (row + (rb - (WIN_BLOCKS - 1)) * BLK) >> 6
        dchunk = qchunk - kchunk
        visible = (dchunk >= 0) & (dchunk <= LEFT_CHUNKS)
        bm_ref[rb * BLK:(rb + 1) * BLK, :] = jnp.where(visible, bias, NEG_BIG)


def _attn_kernel(q_ref, k_ref, v_ref, qc_ref, kc_ref, vc_ref, lines_ref, o_ref, oc_ref,
                 z_scr, u_scr, w_scr, bm_scr, p_scr):
    qi = pl.program_id(2)

    @pl.when(qi == 0)
    def _():
        _build_bias_table(lines_ref, bm_scr)

    kb0 = jnp.maximum(qi - (WIN_BLOCKS - 1), 0)
    row0 = (WIN_BLOCKS - 1 - (qi - kb0)) * BLK
    qc_t = qc_ref[0]
    scores = []
    for j in range(WIN_BLOCKS):
        start = pl.multiple_of(row0 + j * BLK, BLK)
        sc = jnp.dot(kc_ref[0, 0, kb0 + j], qc_t, preferred_element_type=F32)
        scores.append(sc + bm_scr[pl.ds(start, BLK), :])
    m = jnp.max(scores[0], axis=0, keepdims=True)
    for sc in scores[1:]:
        m = jnp.maximum(m, jnp.max(sc, axis=0, keepdims=True))
    denom = jnp.zeros((1, BLK), F32)
    for j in range(WIN_BLOCKS):
        p = jnp.exp(scores[j] - m)
        denom = denom + jnp.sum(p, axis=0, keepdims=True)
        p_scr[j] = p.astype(BF16)

    last = k_ref.shape[2] - 1
    q_t = q_ref[0]
    sub = lax.broadcasted_iota(jnp.int32, (SUBLANES, BLK), 0)
    col = lax.broadcasted_iota(jnp.int32, (SUBLANES, BLK), 1)
    diag_cols = col - ROWGROUPS * sub - (ROWGROUPS - 1)

    def logits(kb):
        return jnp.dot(k_ref[0, 0, kb], q_t, preferred_element_type=F32)

    z_scr[...] = logits(qi)
    w_scr[...] = jnp.zeros_like(w_scr)
    offset, carry = _stick_scan(z_scr, u_scr, jnp.zeros((SUBLANES, BLK), F32), diag_cols)
    z_scr[...] = logits(jnp.maximum(qi - 1, 0))

    def body(t, state):
        acc, offset, carry = state
        kb = qi - 1 - t
        pv = jnp.dot(v_ref[0, 0, jnp.minimum(kb + 2, last)], w_scr[...], preferred_element_type=F32)
        offset, carry = _stick_scan(z_scr, u_scr, carry, None, w_scr, offset)
        z_scr[...] = logits(jnp.maximum(kb - 1, 0))
        return acc + pv, offset, carry

    acc, offset, _ = lax.fori_loop(0, qi, body, (jnp.zeros((HEAD_DIM, BLK), F32), offset, carry))
    acc = acc + jnp.dot(v_ref[0, 0, min(1, last)], w_scr[...], preferred_element_type=F32)
    acc = acc + jnp.dot(v_ref[0, 0, 0], _stick_exp(u_scr, offset), preferred_element_type=F32)
    o_ref[0] = acc.astype(BF16)

    out = jnp.zeros((HEAD_DIM, BLK), F32)
    for j in range(WIN_BLOCKS):
        out = out + jnp.dot(vc_ref[0, 0, kb0 + j], p_scr[j], preferred_element_type=F32)
    oc_ref[0] = (out / denom).astype(BF16)


def _attn_call(qa_t, ka, va_t, qc_t, kc, vc_t, rel_bias):
    b, _, s = qa_t.shape
    nb = s // BLK
    lines = _bias_lines(rel_bias)
    qspec = pl.BlockSpec((1, HEAD_DIM, BLK), lambda bi, h, qi: (bi, h, qi))
    kspec = pl.BlockSpec((1, 1, nb, BLK, HEAD_DIM), lambda bi, h, qi: (bi, h, 0, 0, 0))
    vspec = pl.BlockSpec((1, 1, nb, HEAD_DIM, BLK), lambda bi, h, qi: (bi, h, 0, 0, 0))
    yshape = jax.ShapeDtypeStruct((b, D_GRP, s), BF16)
    return pl.pallas_call(
        _attn_kernel,
        grid=(b, N_HEADS, nb),
        in_specs=[qspec, kspec, vspec, qspec, kspec, vspec,
                  pl.BlockSpec((1,) + lines.shape[1:], lambda bi, h, qi: (h, 0, 0))],
        out_specs=[qspec, qspec],
        out_shape=[yshape, yshape],
        scratch_shapes=[pltpu.VMEM((BLK, BLK), F32), pltpu.VMEM((BLK, BLK), F32),
                        pltpu.VMEM((BLK, BLK), BF16), pltpu.VMEM((TABLE_ROWS, BLK), F32),
                        pltpu.VMEM((WIN_BLOCKS, BLK, BLK), BF16)],
        compiler_params=pltpu.CompilerParams(
            dimension_semantics=("parallel", "parallel", "arbitrary"), vmem_limit_bytes=VMEM_LIMIT),
        name="attn",
    )(qa_t, ka, va_t, qc_t, kc, vc_t, lines)


def _bias_lines(rel_bias):
    m = np.arange(2 * BLK)[None, :]
    diff = np.where(m < BLK, m, m - 2 * BLK)
    rel = diff + ((WIN_BLOCKS - 1) - np.arange(TABLE_ROWS // BLK)[:, None]) * BLK
    return rel_bias[:, np.clip(rel, -REL_CLIP, REL_CLIP) + REL_CLIP].astype(F32)


def _layer_norm(v, g, b):
    mu = jnp.mean(v, axis=-1, keepdims=True)
    c = v - mu
    var = jnp.mean(c * c, axis=-1, keepdims=True)
    return c * lax.rsqrt(var + LN_EPS) * g + b


def _post_kernel(alpha, ya_ref, yc_ref, x_ref, woa_ref, woc_ref, g_ref, b_ref, wrh_ref, wrl_ref,
                 br_ref, h_ref, hb_ref, cw_ref):
    mix = _tn_dot(ya_ref[0], woa_ref[...]) + _tn_dot(yc_ref[0], woc_ref[...])
    h = _layer_norm(alpha * x_ref[0] + mix, g_ref[...], b_ref[...])
    h_ref[...] = h
    hb_ref[...] = h.astype(BF16)

    h_hi = h.astype(BF16)
    h_lo = (h - h_hi.astype(F32)).astype(BF16)
    logits = (_nt_dot(wrh_ref[...], h_hi) + _nt_dot(wrh_ref[...], h_lo)
              + _nt_dot(wrl_ref[...], h_hi) + br_ref[...])[0:ROUTER_ROWS]
    row = lax.broadcasted_iota(jnp.int32, logits.shape, 0)
    big = jnp.int32(LANES)
    is_g = (row >= N_EXPERTS) & (row < N_EXPERTS + N_GROUPS)
    gl = jnp.where(is_g, logits, NEG_BIG)
    gmax = jnp.max(gl, axis=0, keepdims=True)
    g_gate = 1.0 / jnp.sum(jnp.exp(gl - gmax), axis=0, keepdims=True)
    g_idx = jnp.min(jnp.where(gl == gmax, row, big), axis=0, keepdims=True) - N_EXPERTS
    in_grp = (row >= g_idx * EXPERTS_PER_GROUP) & (row < (g_idx + 1) * EXPERTS_PER_GROUP)
    el = jnp.where(in_grp, logits, NEG_BIG)
    e1 = jnp.max(el, axis=0, keepdims=True)
    i1 = jnp.min(jnp.where(el == e1, row, big), axis=0, keepdims=True)
    el2 = jnp.where(row == i1, NEG_BIG, el)
    e2 = jnp.max(el2, axis=0, keepdims=True)
    i2 = jnp.min(jnp.where(el2 == e2, row, big), axis=0, keepdims=True)
    r = jnp.exp(e2 - e1)
    gate1 = g_gate / (1.0 + r)
    gate2 = gate1 * r
    cw_t = jnp.where(row == i1, gate1,
                     jnp.where(row == i2, gate2,
                               jnp.where(row == g_idx + N_EXPERTS, 1.0, 0.0)))
    cw_t = jnp.concatenate([cw_t, jnp.zeros((LANES - ROUTER_ROWS, cw_t.shape[1]), F32)], axis=0)
    cw_ref[...] = cw_t.T


def _post_call(ya_t, yc_t, x, woa, woc, g, b, wr, br, alpha):
    bsz, s, d = x.shape
    nb = s // BLK
    n = bsz * s
    wrh = wr.astype(BF16)
    wrl = (wr - wrh.astype(F32)).astype(BF16)
    row = lambda bi, t: (bi * nb + t, 0)
    full = lambda a: pl.BlockSpec(a.shape, lambda bi, t: (0,) * a.ndim)
    return pl.pallas_call(
        functools.partial(_post_kernel, alpha),
        grid=(bsz, nb),
        in_specs=[pl.BlockSpec((1, D_GRP, BLK), lambda bi, t: (bi, 0, t)),
                  pl.BlockSpec((1, D_GRP, BLK), lambda bi, t: (bi, 0, t)),
                  pl.BlockSpec((1, BLK, d), lambda bi, t: (bi, t, 0)),
                  full(woa), full(woc), full(g), full(b), full(wrh), full(wrl), full(br)],
        out_specs=[pl.BlockSpec((BLK, d), row), pl.BlockSpec((BLK, d), row),
                   pl.BlockSpec((BLK, LANES), row)],
        out_shape=[jax.ShapeDtypeStruct((n, d), F32), jax.ShapeDtypeStruct((n, d), BF16),
                   jax.ShapeDtypeStruct((n, LANES), F32)],
        compiler_params=pltpu.CompilerParams(
            dimension_semantics=("parallel", "parallel"), vmem_limit_bytes=VMEM_LIMIT),
        name="post",
    )(ya_t, yc_t, x, woa, woc, g, b, wrh, wrl, br)


MOE_T = 1024
MOE_CH = 128
MOE_VMEM_LIMIT = 60000 * 1024


def _moe_kernel(alpha, hb_ref, h_ref, cw_ref, wg_ref, wu_ref, wd_ref, g_ref, b_ref, o_ref, tri_ref):
    ti = pl.program_id(0)
    g = pl.program_id(1)

    @pl.when((ti == 0) & (g == 0))
    def _():
        r = lax.broadcasted_iota(jnp.int32, (MOE_T, MOE_T), 0)
        c = lax.broadcasted_iota(jnp.int32, (MOE_T, MOE_T), 1)
        tri_ref[...] = (r < c).astype(BF16)

    @pl.when(g == 0)
    def _():
        o_ref[...] = jnp.zeros_like(o_ref)

    cwt = cw_ref[...]
    lane = lax.broadcasted_iota(jnp.int32, (MOE_T, LANES), 1)
    sr = lax.broadcasted_iota(jnp.int32, (LANES, LANES), 0)
    sl = lax.broadcasted_iota(jnp.int32, (LANES, LANES), 1)
    sel = ((sl == sr + N_EXPERTS) & (sr < N_GROUPS)).astype(BF16)
    member = _nt_dot(sel, cwt.astype(BF16))
    rank = jnp.dot(member.astype(BF16), tri_ref[...], preferred_element_type=F32)
    rowi = lax.broadcasted_iota(jnp.int32, (LANES, MOE_T), 0)
    in_row = jnp.sum(jnp.where(rowi == g, member, 0.0), axis=0, keepdims=True)
    rank_row = jnp.sum(jnp.where(rowi == g, rank, 0.0), axis=0, keepdims=True)
    rank_col = jnp.sum(jnp.where(lane == g, rank.T, 0.0), axis=1, keepdims=True)
    in_col = jnp.sum(jnp.where(lane == g + N_EXPERTS, cwt, 0.0), axis=1, keepdims=True)
    count = jnp.sum(in_row).astype(jnp.int32)
    c1 = cwt.astype(BF16)
    rem = cwt - c1.astype(F32)
    c2 = rem.astype(BF16)
    c3 = (rem - c2.astype(F32)).astype(BF16)
    hb = hb_ref[...]
    df = wg_ref.shape[2] // EXPERTS_PER_GROUP
    chunk_row = lax.broadcasted_iota(jnp.int32, (MOE_CH, MOE_T), 0).astype(F32)
    chunk_lane = lax.broadcasted_iota(jnp.int32, (MOE_T, MOE_CH), 1).astype(F32)
    lane_c = lax.broadcasted_iota(jnp.int32, (MOE_CH, LANES), 1)

    def body(c, carry):
        base = (c * MOE_CH).astype(F32)
        p = ((rank_row == base + chunk_row) & (in_row > 0.0)).astype(BF16)
        x = jnp.dot(p, hb, preferred_element_type=F32).astype(BF16)
        cwc = (jnp.dot(p, c1, preferred_element_type=F32)
               + jnp.dot(p, c2, preferred_element_type=F32)
               + jnp.dot(p, c3, preferred_element_type=F32))
        gate = jnp.dot(x, wg_ref[0], preferred_element_type=F32)
        up = jnp.dot(x, wu_ref[0], preferred_element_type=F32)
        hid = []
        for e in range(EXPERTS_PER_GROUP):
            col = jnp.sum(jnp.where(lane_c == g * EXPERTS_PER_GROUP + e, cwc, 0.0),
                          axis=1, keepdims=True)
            ge = gate[:, e * df:(e + 1) * df]
            hid.append((ge * (1.0 / (1.0 + jnp.exp(-ge))) * up[:, e * df:(e + 1) * df] * col)
                       .astype(BF16))
        out = jnp.dot(jnp.concatenate(hid, axis=1), wd_ref[0], preferred_element_type=F32)
        p_t = ((rank_col == base + chunk_lane) & (in_col > 0.0)).astype(BF16)
        o_ref[...] += jnp.dot(p_t, out.astype(BF16), preferred_element_type=F32)
        return carry

    lax.fori_loop(0, (count + MOE_CH - 1) // MOE_CH, body, 0)

    @pl.when(g == pl.num_programs(1) - 1)
    def _():
        o_ref[...] = _layer_norm(alpha * h_ref[...] + o_ref[...], g_ref[...], b_ref[...])


def _moe_call(hb, h, cw, wg, wu, wd, g, b, alpha):
    n, d = h.shape
    ng, _, dh = wg.shape
    tok = lambda t, q: (t, 0)
    full = lambda a: pl.BlockSpec(a.shape, lambda t, q: (0,) * a.ndim)
    return pl.pallas_call(
        functools.partial(_moe_kernel, alpha),
        grid=(n // MOE_T, ng),
        in_specs=[pl.BlockSpec((MOE_T, d), tok), pl.BlockSpec((MOE_T, d), tok),
                  pl.BlockSpec((MOE_T, LANES), tok),
                  pl.BlockSpec((1, d, dh), lambda t, q: (q, 0, 0)),
                  pl.BlockSpec((1, d, dh), lambda t, q: (q, 0, 0)),
                  pl.BlockSpec((1, dh, d), lambda t, q: (q, 0, 0)),
                  full(g), full(b)],
        out_specs=pl.BlockSpec((MOE_T, d), tok),
        out_shape=jax.ShapeDtypeStruct((n, d), F32),
        scratch_shapes=[pltpu.VMEM((MOE_T, MOE_T), BF16)],
        compiler_params=pltpu.CompilerParams(
            dimension_semantics=("arbitrary", "arbitrary"), vmem_limit_bytes=MOE_VMEM_LIMIT),
        name="moe",
    )(hb, h, cw, wg, wu, wd, g, b)


def _layer(x, w_in, w_out, rel_bias, ln1_g, ln1_b, w_group, b_group, w_expert, b_expert,
           w_gate, w_up, w_down, ln2_g, ln2_b, alpha):
    bsz, s, d = x.shape
    scale = HEAD_DIM ** -0.5
    wqa, wka, wva, wqc, wkc, wvc = [w_in[:, i * D_GRP:(i + 1) * D_GRP] for i in range(6)]
    wnat = jnp.concatenate([wqa * (scale * LOG2E), wqc * scale, wvc], axis=1).T.astype(BF16)
    qa_t, qc_t, vc_t, va_t, ka, kc = _proj_call(
        x, wnat, wva.T.astype(BF16), wka.astype(BF16), wkc.astype(BF16))

    ya_t, yc_t = _attn_call(qa_t, ka, va_t, qc_t, kc, vc_t, rel_bias)

    wo = w_out.astype(BF16)
    wr = jnp.zeros((LANES, d), F32)
    wr = wr.at[:N_EXPERTS].set(w_expert.transpose(0, 2, 1).reshape(N_EXPERTS, d))
    wr = wr.at[N_EXPERTS:N_EXPERTS + N_GROUPS].set(w_group.T)
    br = jnp.zeros((LANES, 1), F32)
    br = br.at[:N_EXPERTS, 0].set(b_expert.reshape(N_EXPERTS))
    br = br.at[N_EXPERTS:N_EXPERTS + N_GROUPS, 0].set(b_group)
    h, hb, cw = _post_call(ya_t, yc_t, x, wo[:D_GRP], wo[D_GRP:], ln1_g[None], ln1_b[None],
                           wr, br, alpha)

    df = w_gate.shape[-1]
    by_group = lambda w: w.astype(BF16).transpose(0, 2, 1, 3).reshape(
        N_GROUPS, d, EXPERTS_PER_GROUP * df)
    out = _moe_call(hb, h, cw, by_group(w_gate), by_group(w_up),
                    w_down.astype(BF16).reshape(N_GROUPS, EXPERTS_PER_GROUP * df, d),
                    ln2_g[None], ln2_b[None], alpha)
    return out.reshape(bsz, s, d)


def kernel(x, w_in, w_out, rel_bias, ln1_g, ln1_b, w_group, b_group, w_expert, b_expert,
           w_gate, w_up, w_down, ln2_g, ln2_b):
    depth = w_in.shape[0]
    alpha = (2.0 * depth) ** 0.25
    for l in range(depth):
        x = _layer(x, w_in[l], w_out[l], rel_bias[l], ln1_g[l], ln1_b[l], w_group[l], b_group[l],
                   w_expert[l], b_expert[l], w_gate[l], w_up[l], w_down[l], ln2_g[l], ln2_b[l], alpha)
    return x
```

```python
import functools

import numpy as np
import jax
import jax.numpy as jnp
from jax import lax
from jax.experimental import pallas as pl
from jax.experimental.pallas import tpu as pltpu

F32 = jnp.float32
BF16 = jnp.bfloat16

HEAD_DIM = 64
N_HEADS = 8
D_GRP = N_HEADS * HEAD_DIM
CHUNK = 64
LEFT_CHUNKS = 8
REL_CLIP = 128
N_GROUPS = 4
EXPERTS_PER_GROUP = 8
N_EXPERTS = N_GROUPS * EXPERTS_PER_GROUP
ROUTER_ROWS = 40
LN_EPS = 1e-5
NEG_BIG = -1e30
LOG2E = float(np.log2(np.e))

SUBLANES = 8
LANES = 128
BLK = 256
ROWGROUPS = BLK // SUBLANES
VMEM_LIMIT = 56 * 1024 * 1024


def _nt_dot(a, b):
    return lax.dot_general(a, b, (((1,), (1,)), ((), ())), preferred_element_type=F32)


def _tn_dot(a, b):
    return lax.dot_general(a, b, (((0,), (0,)), ((), ())), preferred_element_type=F32)


def _proj_kernel(x_ref, perm_ref, wnat_ref, wva_ref, wka_ref, wkc_ref,
                 qa_ref, qc_ref, vc_ref, va_ref, ka_ref, kc_ref):
    xb = x_ref[0].astype(BF16)
    xp = jnp.dot(perm_ref[...], xb, preferred_element_type=F32).astype(BF16)
    nat = _nt_dot(wnat_ref[...], xb)
    qa_ref[0] = nat[0:D_GRP].astype(BF16)
    qc_ref[0] = nat[D_GRP:2 * D_GRP].astype(BF16)
    vc = nat[2 * D_GRP:3 * D_GRP].astype(BF16)
    va = _nt_dot(wva_ref[...], xp).astype(BF16)
    ka = jnp.dot(xp, wka_ref[...], preferred_element_type=F32).astype(BF16)
    kc = jnp.dot(xb, wkc_ref[...], preferred_element_type=F32).astype(BF16)
    for h in range(N_HEADS):
        sl = slice(h * HEAD_DIM, (h + 1) * HEAD_DIM)
        vc_ref[0, h, 0] = vc[sl]
        va_ref[0, h, 0] = va[sl]
        ka_ref[0, h, 0] = ka[:, sl]
        kc_ref[0, h, 0] = kc[:, sl]


def _key_permutation():
    r = np.arange(BLK)
    src = (r % SUBLANES) * ROWGROUPS + (ROWGROUPS - 1 - r // SUBLANES)
    return jnp.asarray(np.eye(BLK, dtype=np.float32)[src], dtype=BF16)


def _proj_call(x, wnat, wva, wka, wkc):
    b, s, d = x.shape
    nb = s // BLK
    perm = _key_permutation()
    qspec = pl.BlockSpec((1, D_GRP, BLK), lambda bi, t: (bi, 0, t))
    vspec = pl.BlockSpec((1, N_HEADS, 1, HEAD_DIM, BLK), lambda bi, t: (bi, 0, t, 0, 0))
    kspec = pl.BlockSpec((1, N_HEADS, 1, BLK, HEAD_DIM), lambda bi, t: (bi, 0, t, 0, 0))
    qshape = jax.ShapeDtypeStruct((b, D_GRP, s), BF16)
    vshape = jax.ShapeDtypeStruct((b, N_HEADS, nb, HEAD_DIM, BLK), BF16)
    kshape = jax.ShapeDtypeStruct((b, N_HEADS, nb, BLK, HEAD_DIM), BF16)
    full = lambda a: pl.BlockSpec(a.shape, lambda bi, t: (0,) * a.ndim)
    return pl.pallas_call(
        _proj_kernel,
        grid=(b, nb),
        in_specs=[pl.BlockSpec((1, BLK, d), lambda bi, t: (bi, t, 0)),
                  full(perm), full(wnat), full(wva), full(wka), full(wkc)],
        out_specs=[qspec, qspec, vspec, vspec, kspec, kspec],
        out_shape=[qshape, qshape, vshape, vshape, kshape, kshape],
        compiler_params=pltpu.CompilerParams(
            dimension_semantics=("parallel", "parallel"), vmem_limit_bytes=VMEM_LIMIT),
        name="proj",
    )(x, perm, wnat, wva, wka, wkc)


def _sublane_shift_up(t, d, sub):
    rolled = pltpu.roll(t, SUBLANES - d, axis=0)
    return jnp.where(sub < SUBLANES - d, rolled, 0.0)


def _stick_scan(z_ref, u_ref, carry, diag_cols, w_ref=None, prev_offset=None):
    n = z_ref.shape[1]
    sub = lax.broadcasted_iota(jnp.int32, (SUBLANES, n), 0)
    run = jnp.zeros((SUBLANES, n), F32)
    prev_w = None
    for i in range(ROWGROUPS):
        rows = slice(i * SUBLANES, (i + 1) * SUBLANES)
        if w_ref is not None:
            w = jnp.exp2(u_ref[rows, :] - prev_offset)
            if i % 2 == 0:
                prev_w = w
            else:
                w_ref[(i - 1) * SUBLANES:(i + 1) * SUBLANES, :] = jnp.concatenate(
                    [prev_w, w], axis=0).astype(BF16)
        zi = z_ref[rows, :]
        sp = jnp.maximum(zi, 0.0) + jnp.log2(1.0 + jnp.exp2(-jnp.abs(zi)))
        if diag_cols is None:
            run = run + sp
            u_ref[rows, :] = zi - run
        else:
            causal = diag_cols > -i
            run = run + jnp.where(causal, sp, 0.0)
            u_ref[rows, :] = jnp.where(causal, zi - run, NEG_BIG)
    s1 = run + _sublane_shift_up(run, 1, sub)
    s2 = s1 + _sublane_shift_up(s1, 2, sub)
    s3 = s2 + _sublane_shift_up(s2, 4, sub)
    offset = (s3 - run) + carry
    new_carry = jnp.broadcast_to(s3[0:1, :], (SUBLANES, n)) + carry
    return offset, new_carry


def _stick_exp(u_ref, offset):
    rows = [jnp.exp2(u_ref[i * SUBLANES:(i + 1) * SUBLANES, :] - offset) for i in range(ROWGROUPS)]
    return jnp.concatenate(rows, axis=0).astype(BF16)


WIN_BLOCKS = LEFT_CHUNKS * CHUNK // BLK + 1
TABLE_ROWS = (2 * WIN_BLOCKS - 1) * BLK


def _build_bias_table(lines_ref, bm_ref):
    row = lax.broadcasted_iota(jnp.int32, (BLK, BLK), 0)
    qchunk = lax.broadcasted_iota(jnp.int32, (BLK, BLK), 1) // CHUNK
    for rb in range(TABLE_ROWS // BLK):
        line = jnp.broadcast_to(lines_ref[0, rb:rb + 1, :], (BLK, 2 * BLK))
        bias = pltpu.roll(line, 0, 1, stride=1, stride_axis=0)[:, :BLK]
        kchunk = (row + (rb - (WIN_BLOCKS - 1)) * BLK) >> 6
        dchunk = qchunk - kchunk
        visible = (dchunk >= 0) & (dchunk <= LEFT_CHUNKS)
        bm_ref[rb * BLK:(rb + 1) * BLK, :] = jnp.where(visible, bias, NEG_BIG)


def _attn_kernel(q_ref, k_ref, v_ref, qc_ref, kc_ref, vc_ref, lines_ref, o_ref, oc_ref,
                 z_scr, u_scr, w_scr, bm_scr, p_scr):
    qi = pl.program_id(2)

    @pl.when(qi == 0)
    def _():
        _build_bias_table(lines_ref, bm_scr)

    kb0 = jnp.maximum(qi - (WIN_BLOCKS - 1), 0)
    row0 = (WIN_BLOCKS - 1 - (qi - kb0)) * BLK
    qc_t = qc_ref[0]
    scores = []
    for j in range(WIN_BLOCKS):
        start = pl.multiple_of(row0 + j * BLK, BLK)
        sc = jnp.dot(kc_ref[0, 0, kb0 + j], qc_t, preferred_element_type=F32)
        scores.append(sc + bm_scr[pl.ds(start, BLK), :])
    m = jnp.max(scores[0], axis=0, keepdims=True)
    for sc in scores[1:]:
        m = jnp.maximum(m, jnp.max(sc, axis=0, keepdims=True))
    denom = jnp.zeros((1, BLK), F32)
    for j in range(WIN_BLOCKS):
        p = jnp.exp(scores[j] - m)
        denom = denom + jnp.sum(p, axis=0, keepdims=True)
        p_scr[j] = p.astype(BF16)

    last = k_ref.shape[2] - 1
    q_t = q_ref[0]
    sub = lax.broadcasted_iota(jnp.int32, (SUBLANES, BLK), 0)
    col = lax.broadcasted_iota(jnp.int32, (SUBLANES, BLK), 1)
    diag_cols = col - ROWGROUPS * sub - (ROWGROUPS - 1)

    def logits(kb):
        return jnp.dot(k_ref[0, 0, kb], q_t, preferred_element_type=F32)

    z_scr[...] = logits(qi)
    w_scr[...] = jnp.zeros_like(w_scr)
    offset, carry = _stick_scan(z_scr, u_scr, jnp.zeros((SUBLANES, BLK), F32), diag_cols)
    z_scr[...] = logits(jnp.maximum(qi - 1, 0))

    def body(t, state):
        acc, offset, carry = state
        kb = qi - 1 - t
        pv = jnp.dot(v_ref[0, 0, jnp.minimum(kb + 2, last)], w_scr[...], preferred_element_type=F32)
        offset, carry = _stick_scan(z_scr, u_scr, carry, None, w_scr, offset)
        z_scr[...] = logits(jnp.maximum(kb - 1, 0))
        return acc + pv, offset, carry

    state = (jnp.zeros((HEAD_DIM, BLK), F32), offset, carry)
    odd = qi % 2
    state = lax.cond(odd == 1, lambda s: body(0, s), lambda s: s, state)

    def body_pair(p, s):
        t = odd + 2 * p
        return body(t + 1, body(t, s))

    acc, offset, _ = lax.fori_loop(0, qi // 2, body_pair, state)
    acc = acc + jnp.dot(v_ref[0, 0, min(1, last)], w_scr[...], preferred_element_type=F32)
    acc = acc + jnp.dot(v_ref[0, 0, 0], _stick_exp(u_scr, offset), preferred_element_type=F32)
    o_ref[0] = acc.astype(BF16)

    out = jnp.zeros((HEAD_DIM, BLK), F32)
    for j in range(WIN_BLOCKS):
        out = out + jnp.dot(vc_ref[0, 0, kb0 + j], p_scr[j], preferred_element_type=F32)
    oc_ref[0] = (out / denom).astype(BF16)


def _attn_call(qa_t, ka, va_t, qc_t, kc, vc_t, rel_bias):
    b, _, s = qa_t.shape
    nb = s // BLK
    lines = _bias_lines(rel_bias)
    qspec = pl.BlockSpec((1, HEAD_DIM, BLK), lambda bi, h, qi: (bi, h, qi))
    kspec = pl.BlockSpec((1, 1, nb, BLK, HEAD_DIM), lambda bi, h, qi: (bi, h, 0, 0, 0))
    vspec = pl.BlockSpec((1, 1, nb, HEAD_DIM, BLK), lambda bi, h, qi: (bi, h, 0, 0, 0))
    yshape = jax.ShapeDtypeStruct((b, D_GRP, s), BF16)
    return pl.pallas_call(
        _attn_kernel,
        grid=(b, N_HEADS, nb),
        in_specs=[qspec, kspec, vspec, qspec, kspec, vspec,
                  pl.BlockSpec((1,) + lines.shape[1:], lambda bi, h, qi: (h, 0, 0))],
        out_specs=[qspec, qspec],
        out_shape=[yshape, yshape],
        scratch_shapes=[pltpu.VMEM((BLK, BLK), F32), pltpu.VMEM((BLK, BLK), F32),
                        pltpu.VMEM((BLK, BLK), BF16), pltpu.VMEM((TABLE_ROWS, BLK), F32),
                        pltpu.VMEM((WIN_BLOCKS, BLK, BLK), BF16)],
        compiler_params=pltpu.CompilerParams(
            dimension_semantics=("parallel", "parallel", "arbitrary"), vmem_limit_bytes=VMEM_LIMIT),
        name="attn",
    )(qa_t, ka, va_t, qc_t, kc, vc_t, lines)


def _bias_lines(rel_bias):
    m = np.arange(2 * BLK)[None, :]
    diff = np.where(m < BLK, m, m - 2 * BLK)
    rel = diff + ((WIN_BLOCKS - 1) - np.arange(TABLE_ROWS // BLK)[:, None]) * BLK
    return rel_bias[:, np.clip(rel, -REL_CLIP, REL_CLIP) + REL_CLIP].astype(F32)


def _layer_norm(v, g, b):
    mu = jnp.mean(v, axis=-1, keepdims=True)
    c = v - mu
    var = jnp.mean(c * c, axis=-1, keepdims=True)
    return c * lax.rsqrt(var + LN_EPS) * g + b


def _post_kernel(alpha, ya_ref, yc_ref, x_ref, woa_ref, woc_ref, g_ref, b_ref, wrh_ref, wrl_ref,
                 br_ref, h_ref, hb_ref, cw_ref):
    mix = _tn_dot(ya_ref[0], woa_ref[...]) + _tn_dot(yc_ref[0], woc_ref[...])
    h = _layer_norm(alpha * x_ref[0] + mix, g_ref[...], b_ref[...])
    h_ref[...] = h
    hb_ref[...] = h.astype(BF16)

    h_hi = h.astype(BF16)
    h_lo = (h - h_hi.astype(F32)).astype(BF16)
    logits = (_nt_dot(wrh_ref[...], h_hi) + _nt_dot(wrh_ref[...], h_lo)
              + _nt_dot(wrl_ref[...], h_hi) + br_ref[...])[0:ROUTER_ROWS]
    row = lax.broadcasted_iota(jnp.int32, logits.shape, 0)
    big = jnp.int32(LANES)
    is_g = (row >= N_EXPERTS) & (row < N_EXPERTS + N_GROUPS)
    gl = jnp.where(is_g, logits, NEG_BIG)
    gmax = jnp.max(gl, axis=0, keepdims=True)
    g_gate = 1.0 / jnp.sum(jnp.exp(gl - gmax), axis=0, keepdims=True)
    g_idx = jnp.min(jnp.where(gl == gmax, row, big), axis=0, keepdims=True) - N_EXPERTS
    in_grp = (row >= g_idx * EXPERTS_PER_GROUP) & (row < (g_idx + 1) * EXPERTS_PER_GROUP)
    el = jnp.where(in_grp, logits, NEG_BIG)
    e1 = jnp.max(el, axis=0, keepdims=True)
    i1 = jnp.min(jnp.where(el == e1, row, big), axis=0, keepdims=True)
    el2 = jnp.where(row == i1, NEG_BIG, el)
    e2 = jnp.max(el2, axis=0, keepdims=True)
    i2 = jnp.min(jnp.where(el2 == e2, row, big), axis=0, keepdims=True)
    r = jnp.exp(e2 - e1)
    gate1 = g_gate / (1.0 + r)
    gate2 = gate1 * r
    cw_t = jnp.where(row == i1, gate1,
                     jnp.where(row == i2, gate2,
                               jnp.where(row == g_idx + N_EXPERTS, 1.0, 0.0)))
    cw_t = jnp.concatenate([cw_t, jnp.zeros((LANES - ROUTER_ROWS, cw_t.shape[1]), F32)], axis=0)
    cw_ref[...] = cw_t.T


def _post_call(ya_t, yc_t, x, woa, woc, g, b, wr, br, alpha):
    bsz, s, d = x.shape
    nb = s // BLK
    n = bsz * s
    wrh = wr.astype(BF16)
    wrl = (wr - wrh.astype(F32)).astype(BF16)
    row = lambda bi, t: (bi * nb + t, 0)
    full = lambda a: pl.BlockSpec(a.shape, lambda bi, t: (0,) * a.ndim)
    return pl.pallas_call(
        functools.partial(_post_kernel, alpha),
        grid=(bsz, nb),
        in_specs=[pl.BlockSpec((1, D_GRP, BLK), lambda bi, t: (bi, 0, t)),
                  pl.BlockSpec((1, D_GRP, BLK), lambda bi, t: (bi, 0, t)),
                  pl.BlockSpec((1, BLK, d), lambda bi, t: (bi, t, 0)),
                  full(woa), full(woc), full(g), full(b), full(wrh), full(wrl), full(br)],
        out_specs=[pl.BlockSpec((BLK, d), row), pl.BlockSpec((BLK, d), row),
                   pl.BlockSpec((BLK, LANES), row)],
        out_shape=[jax.ShapeDtypeStruct((n, d), F32), jax.ShapeDtypeStruct((n, d), BF16),
                   jax.ShapeDtypeStruct((n, LANES), F32)],
        compiler_params=pltpu.CompilerParams(
            dimension_semantics=("parallel", "parallel"), vmem_limit_bytes=VMEM_LIMIT),
        name="post",
    )(ya_t, yc_t, x, woa, woc, g, b, wrh, wrl, br)


MOE_T = 1024
MOE_CH = 128
MOE_VMEM_LIMIT = 60000 * 1024


def _moe_kernel(alpha, hb_ref, h_ref, cw_ref, wg_ref, wu_ref, wd_ref, g_ref, b_ref, o_ref,
                tri_ref, member_ref, rank_ref, rank_t_ref):
    ti = pl.program_id(0)
    g = pl.program_id(1)

    @pl.when((ti == 0) & (g == 0))
    def _():
        r = lax.broadcasted_iota(jnp.int32, (MOE_T, MOE_T), 0)
        c = lax.broadcasted_iota(jnp.int32, (MOE_T, MOE_T), 1)
        tri_ref[...] = (r < c).astype(BF16)

    cwt = cw_ref[...]
    lane = lax.broadcasted_iota(jnp.int32, (MOE_T, LANES), 1)

    @pl.when(g == 0)
    def _():
        o_ref[...] = jnp.zeros_like(o_ref)
        sr = lax.broadcasted_iota(jnp.int32, (LANES, LANES), 0)
        sl = lax.broadcasted_iota(jnp.int32, (LANES, LANES), 1)
        sel = ((sl == sr + N_EXPERTS) & (sr < N_GROUPS)).astype(BF16)
        member = _nt_dot(sel, cwt.astype(BF16))
        rank = jnp.dot(member.astype(BF16), tri_ref[...], preferred_element_type=F32)
        member_ref[...] = member[0:SUBLANES]
        rank_ref[...] = rank[0:SUBLANES]
        rank_t_ref[...] = rank.T

    rowi = lax.broadcasted_iota(jnp.int32, (SUBLANES, MOE_T), 0)
    in_row = jnp.sum(jnp.where(rowi == g, member_ref[...], 0.0), axis=0, keepdims=True)
    rank_row = jnp.sum(jnp.where(rowi == g, rank_ref[...], 0.0), axis=0, keepdims=True)
    rank_col = jnp.sum(jnp.where(lane == g, rank_t_ref[...], 0.0), axis=1, keepdims=True)
    in_col = jnp.sum(jnp.where(lane == g + N_EXPERTS, cwt, 0.0), axis=1, keepdims=True)
    count = jnp.sum(in_row).astype(jnp.int32)
    c1 = cwt.astype(BF16)
    rem = cwt - c1.astype(F32)
    c2 = rem.astype(BF16)
    c3 = (rem - c2.astype(F32)).astype(BF16)
    hb = hb_ref[...]
    df = wg_ref.shape[2] // EXPERTS_PER_GROUP
    chunk_row = lax.broadcasted_iota(jnp.int32, (MOE_CH, MOE_T), 0).astype(F32)
    chunk_lane = lax.broadcasted_iota(jnp.int32, (MOE_T, MOE_CH), 1).astype(F32)
    lane_c = lax.broadcasted_iota(jnp.int32, (MOE_CH, LANES), 1)

    def body(c, carry):
        base = (c * MOE_CH).astype(F32)
        p = ((rank_row == base + chunk_row) & (in_row > 0.0)).astype(BF16)
        x = jnp.dot(p, hb, preferred_element_type=F32).astype(BF16)
        cwc = (jnp.dot(p, c1, preferred_element_type=F32)
               + jnp.dot(p, c2, preferred_element_type=F32)
               + jnp.dot(p, c3, preferred_element_type=F32))
        gate = jnp.dot(x, wg_ref[0], preferred_element_type=F32)
        up = jnp.dot(x, wu_ref[0], preferred_element_type=F32)
        hid = []
        for e in range(EXPERTS_PER_GROUP):
            col = jnp.sum(jnp.where(lane_c == g * EXPERTS_PER_GROUP + e, cwc, 0.0),
                          axis=1, keepdims=True)
            ge = gate[:, e * df:(e + 1) * df]
            hid.append((ge * (1.0 / (1.0 + jnp.exp(-ge))) * up[:, e * df:(e + 1) * df] * col)
                       .astype(BF16))
        out = jnp.dot(jnp.concatenate(hid, axis=1), wd_ref[0], preferred_element_type=F32)
        p_t = ((rank_col == base + chunk_lane) & (in_col > 0.0)).astype(BF16)
        o_ref[...] += jnp.dot(p_t, out.astype(BF16), preferred_element_type=F32)
        return carry

    lax.fori_loop(0, (count + MOE_CH - 1) // MOE_CH, body, 0)

    @pl.when(g == pl.num_programs(1) - 1)
    def _():
        o_ref[...] = _layer_norm(alpha * h_ref[...] + o_ref[...], g_ref[...], b_ref[...])


def _moe_call(hb, h, cw, wg, wu, wd, g, b, alpha):
    n, d = h.shape
    ng, _, dh = wg.shape
    tok = lambda t, q: (t, 0)
    full = lambda a: pl.BlockSpec(a.shape, lambda t, q: (0,) * a.ndim)
    return pl.pallas_call(
        functools.partial(_moe_kernel, alpha),
        grid=(n // MOE_T, ng),
        in_specs=[pl.BlockSpec((MOE_T, d), tok), pl.BlockSpec((MOE_T, d), tok),
                  pl.BlockSpec((MOE_T, LANES), tok),
                  pl.BlockSpec((1, d, dh), lambda t, q: (q, 0, 0)),
                  pl.BlockSpec((1, d, dh), lambda t, q: (q, 0, 0)),
                  pl.BlockSpec((1, dh, d), lambda t, q: (q, 0, 0)),
                  full(g), full(b)],
        out_specs=pl.BlockSpec((MOE_T, d), tok),
        out_shape=jax.ShapeDtypeStruct((n, d), F32),
        scratch_shapes=[pltpu.VMEM((MOE_T, MOE_T), BF16), pltpu.VMEM((SUBLANES, MOE_T), F32),
                        pltpu.VMEM((SUBLANES, MOE_T), F32), pltpu.VMEM((MOE_T, LANES), F32)],
        compiler_params=pltpu.CompilerParams(
            dimension_semantics=("arbitrary", "arbitrary"), vmem_limit_bytes=MOE_VMEM_LIMIT),
        name="moe",
    )(hb, h, cw, wg, wu, wd, g, b)


def _layer(x, w_in, w_out, rel_bias, ln1_g, ln1_b, w_group, b_group, w_expert, b_expert,
           w_gate, w_up, w_down, ln2_g, ln2_b, alpha):
    bsz, s, d = x.shape
    scale = HEAD_DIM ** -0.5
    wqa, wka, wva, wqc, wkc, wvc = [w_in[:, i * D_GRP:(i + 1) * D_GRP] for i in range(6)]
    wnat = jnp.concatenate([wqa * (scale * LOG2E), wqc * scale, wvc], axis=1).T.astype(BF16)
    qa_t, qc_t, vc_t, va_t, ka, kc = _proj_call(
        x, wnat, wva.T.astype(BF16), wka.astype(BF16), wkc.astype(BF16))

    ya_t, yc_t = _attn_call(qa_t, ka, va_t, qc_t, kc, vc_t, rel_bias)

    wo = w_out.astype(BF16)
    wr = jnp.zeros((LANES, d), F32)
    wr = wr.at[:N_EXPERTS].set(w_expert.transpose(0, 2, 1).reshape(N_EXPERTS, d))
    wr = wr.at[N_EXPERTS:N_EXPERTS + N_GROUPS].set(w_group.T)
    br = jnp.zeros((LANES, 1), F32)
    br = br.at[:N_EXPERTS, 0].set(b_expert.reshape(N_EXPERTS))
    br = br.at[N_EXPERTS:N_EXPERTS + N_GROUPS, 0].set(b_group)
    h, hb, cw = _post_call(ya_t, yc_t, x, wo[:D_GRP], wo[D_GRP:], ln1_g[None], ln1_b[None],
                           wr, br, alpha)

    df = w_gate.shape[-1]
    by_group = lambda w: w.astype(BF16).transpose(0, 2, 1, 3).reshape(
        N_GROUPS, d, EXPERTS_PER_GROUP * df)
    out = _moe_call(hb, h, cw, by_group(w_gate), by_group(w_up),
                    w_down.astype(BF16).reshape(N_GROUPS, EXPERTS_PER_GROUP * df, d),
                    ln2_g[None], ln2_b[None], alpha)
    return out.reshape(bsz, s, d)


def kernel(x, w_in, w_out, rel_bias, ln1_g, ln1_b, w_group, b_group, w_expert, b_expert,
           w_gate, w_up, w_down, ln2_g, ln2_b):
    depth = w_in.shape[0]
    alpha = (2.0 * depth) ** 0.25
    for l in range(depth):
        x = _layer(x, w_in[l], w_out[l], rel_bias[l], ln1_g[l], ln1_b[l], w_group[l], b_group[l],
                   w_expert[l], b_expert[l], w_gate[l], w_up[l], w_down[l], ln2_g[l], ln2_b[l], alpha)
    return x
```

```python
import functools

import numpy as np
import jax
import jax.numpy as jnp
from jax import lax
from jax.experimental import pallas as pl
from jax.experimental.pallas import tpu as pltpu

F32 = jnp.float32
BF16 = jnp.bfloat16

HEAD_DIM = 64
N_HEADS = 8
D_GRP = N_HEADS * HEAD_DIM
CHUNK = 64
LEFT_CHUNKS = 8
REL_CLIP = 128
N_GROUPS = 4
EXPERTS_PER_GROUP = 8
N_EXPERTS = N_GROUPS * EXPERTS_PER_GROUP
ROUTER_ROWS = 40
LN_EPS = 1e-5
NEG_BIG = -1e30
LOG2E = float(np.log2(np.e))

SUBLANES = 8
LANES = 128
BLK = 256
ROWGROUPS = BLK // SUBLANES
VMEM_LIMIT = 56 * 1024 * 1024


def _nt_dot(a, b):
    return lax.dot_general(a, b, (((1,), (1,)), ((), ())), preferred_element_type=F32)


def _tn_dot(a, b):
    return lax.dot_general(a, b, (((0,), (0,)), ((), ())), preferred_element_type=F32)


def _proj_kernel(x_ref, perm_ref, wnat_ref, wva_ref, wka_ref, wkc_ref,
                 qa_ref, qc_ref, vc_ref, va_ref, ka_ref, kc_ref):
    xb = x_ref[0].astype(BF16)
    xp = jnp.dot(perm_ref[...], xb, preferred_element_type=F32).astype(BF16)
    nat = _nt_dot(wnat_ref[...], xb)
    qa_ref[0] = nat[0:D_GRP].astype(BF16)
    qc_ref[0] = nat[D_GRP:2 * D_GRP].astype(BF16)
    vc = nat[2 * D_GRP:3 * D_GRP].astype(BF16)
    va = _nt_dot(wva_ref[...], xp).astype(BF16)
    ka = jnp.dot(xp, wka_ref[...], preferred_element_type=F32).astype(BF16)
    kc = jnp.dot(xb, wkc_ref[...], preferred_element_type=F32).astype(BF16)
    for h in range(N_HEADS):
        sl = slice(h * HEAD_DIM, (h + 1) * HEAD_DIM)
        vc_ref[0, h, 0] = vc[sl]
        va_ref[0, h, 0] = va[sl]
        ka_ref[0, h, 0] = ka[:, sl]
        kc_ref[0, h, 0] = kc[:, sl]


def _key_permutation():
    r = np.arange(BLK)
    src = (r % SUBLANES) * ROWGROUPS + (ROWGROUPS - 1 - r // SUBLANES)
    return jnp.asarray(np.eye(BLK, dtype=np.float32)[src], dtype=BF16)


def _proj_call(x, wnat, wva, wka, wkc):
    b, s, d = x.shape
    nb = s // BLK
    perm = _key_permutation()
    qspec = pl.BlockSpec((1, D_GRP, BLK), lambda bi, t: (bi, 0, t))
    vspec = pl.BlockSpec((1, N_HEADS, 1, HEAD_DIM, BLK), lambda bi, t: (bi, 0, t, 0, 0))
    kspec = pl.BlockSpec((1, N_HEADS, 1, BLK, HEAD_DIM), lambda bi, t: (bi, 0, t, 0, 0))
    qshape = jax.ShapeDtypeStruct((b, D_GRP, s), BF16)
    vshape = jax.ShapeDtypeStruct((b, N_HEADS, nb, HEAD_DIM, BLK), BF16)
    kshape = jax.ShapeDtypeStruct((b, N_HEADS, nb, BLK, HEAD_DIM), BF16)
    full = lambda a: pl.BlockSpec(a.shape, lambda bi, t: (0,) * a.ndim)
    return pl.pallas_call(
        _proj_kernel,
        grid=(b, nb),
        in_specs=[pl.BlockSpec((1, BLK, d), lambda bi, t: (bi, t, 0)),
                  full(perm), full(wnat), full(wva), full(wka), full(wkc)],
        out_specs=[qspec, qspec, vspec, vspec, kspec, kspec],
        out_shape=[qshape, qshape, vshape, vshape, kshape, kshape],
        compiler_params=pltpu.CompilerParams(
            dimension_semantics=("parallel", "parallel"), vmem_limit_bytes=VMEM_LIMIT),
        name="proj",
    )(x, perm, wnat, wva, wka, wkc)


def _sublane_shift_up(t, d, sub):
    rolled = pltpu.roll(t, SUBLANES - d, axis=0)
    return jnp.where(sub < SUBLANES - d, rolled, 0.0)


def _stick_scan(z_ref, u_ref, carry, diag_cols, w_ref=None, prev_offset=None):
    n = z_ref.shape[1]
    sub = lax.broadcasted_iota(jnp.int32, (SUBLANES, n), 0)
    run = jnp.zeros((SUBLANES, n), F32)
    prev_w = None
    for i in range(ROWGROUPS):
        rows = slice(i * SUBLANES, (i + 1) * SUBLANES)
        if w_ref is not None:
            w = jnp.exp2(u_ref[rows, :] - prev_offset)
            if i % 2 == 0:
                prev_w = w
            else:
                w_ref[(i - 1) * SUBLANES:(i + 1) * SUBLANES, :] = jnp.concatenate(
                    [prev_w, w], axis=0).astype(BF16)
        zi = z_ref[rows, :]
        sp = jnp.maximum(zi, 0.0) + jnp.log2(1.0 + jnp.exp2(-jnp.abs(zi)))
        if diag_cols is None:
            run = run + sp
            u_ref[rows, :] = zi - run
        else:
            causal = diag_cols > -i
            run = run + jnp.where(causal, sp, 0.0)
            u_ref[rows, :] = jnp.where(causal, zi - run, NEG_BIG)
    s1 = run + _sublane_shift_up(run, 1, sub)
    s2 = s1 + _sublane_shift_up(s1, 2, sub)
    s3 = s2 + _sublane_shift_up(s2, 4, sub)
    offset = (s3 - run) + carry
    new_carry = jnp.broadcast_to(s3[0:1, :], (SUBLANES, n)) + carry
    return offset, new_carry


def _stick_exp(u_ref, offset):
    rows = [jnp.exp2(u_ref[i * SUBLANES:(i + 1) * SUBLANES, :] - offset) for i in range(ROWGROUPS)]
    return jnp.concatenate(rows, axis=0).astype(BF16)


WIN_BLOCKS = LEFT_CHUNKS * CHUNK // BLK + 1
TABLE_ROWS = (2 * WIN_BLOCKS - 1) * BLK


def _build_bias_table(lines_ref, bm_ref):
    row = lax.broadcasted_iota(jnp.int32, (BLK, BLK), 0)
    qchunk = lax.broadcasted_iota(jnp.int32, (BLK, BLK), 1) // CHUNK
    for rb in range(TABLE_ROWS // BLK):
        line = jnp.broadcast_to(lines_ref[rb:rb + 1, :], (BLK, 2 * BLK))
        bias = pltpu.roll(line, 0, 1, stride=1, stride_axis=0)[:, :BLK]
        kchunk = (row + (rb - (WIN_BLOCKS - 1)) * BLK) >> 6
        dchunk = qchunk - kchunk
        visible = (dchunk >= 0) & (dchunk <= LEFT_CHUNKS)
        bm_ref[rb * BLK:(rb + 1) * BLK, :] = jnp.where(visible, bias, NEG_BIG)


def _chunk_scores(qi, q_t, k_ref, bm_ref, p_ref):
    kb0 = jnp.maximum(qi - (WIN_BLOCKS - 1), 0)
    row0 = (WIN_BLOCKS - 1 - (qi - kb0)) * BLK
    scores = []
    for j in range(WIN_BLOCKS):
        start = pl.multiple_of(row0 + j * BLK, BLK)
        sc = jnp.dot(k_ref[kb0 + j], q_t, preferred_element_type=F32)
        scores.append(sc + bm_ref[pl.ds(start, BLK), :])
    m = jnp.max(scores[0], axis=0, keepdims=True)
    for sc in scores[1:]:
        m = jnp.maximum(m, jnp.max(sc, axis=0, keepdims=True))
    denom = jnp.zeros((1, BLK), F32)
    for j in range(WIN_BLOCKS):
        p = jnp.exp(scores[j] - m)
        denom = denom + jnp.sum(p, axis=0, keepdims=True)
        p_ref[j] = p.astype(BF16)
    return kb0, denom


def _stick_head(qi, q_t, k_ref, v_ref, z_scr, u_scr, w_scr):
    last = k_ref.shape[0] - 1
    sub = lax.broadcasted_iota(jnp.int32, (SUBLANES, BLK), 0)
    col = lax.broadcasted_iota(jnp.int32, (SUBLANES, BLK), 1)
    diag_cols = col - ROWGROUPS * sub - (ROWGROUPS - 1)

    def logits(kb):
        return jnp.dot(k_ref[kb], q_t, preferred_element_type=F32)

    z_scr[...] = logits(qi)
    w_scr[...] = jnp.zeros_like(w_scr)
    offset, carry = _stick_scan(z_scr, u_scr, jnp.zeros((SUBLANES, BLK), F32), diag_cols)
    z_scr[...] = logits(jnp.maximum(qi - 1, 0))

    def body(t, state):
        acc, offset, carry = state
        kb = qi - 1 - t
        pv = jnp.dot(v_ref[jnp.minimum(kb + 2, last)], w_scr[...], preferred_element_type=F32)
        offset, carry = _stick_scan(z_scr, u_scr, carry, None, w_scr, offset)
        z_scr[...] = logits(jnp.maximum(kb - 1, 0))
        return acc + pv, offset, carry

    state = (jnp.zeros((HEAD_DIM, BLK), F32), offset, carry)
    odd = qi % 2
    state = lax.cond(odd == 1, lambda s: body(0, s), lambda s: s, state)

    def body_pair(p, s):
        t = odd + 2 * p
        return body(t + 1, body(t, s))

    acc, offset, _ = lax.fori_loop(0, qi // 2, body_pair, state)
    acc = acc + jnp.dot(v_ref[min(1, last)], w_scr[...], preferred_element_type=F32)
    return acc + jnp.dot(v_ref[0], _stick_exp(u_scr, offset), preferred_element_type=F32)


ATTN_HEADS = 2


def _attn_kernel(q_ref, k_ref, v_ref, qc_ref, kc_ref, vc_ref, lines_ref, o_ref, oc_ref,
                 z_scr, u_scr, w_scr, bm_scr, p_scr):
    qi = pl.program_id(2)
    heads = range(ATTN_HEADS)
    rows = lambda hh: slice(hh * HEAD_DIM, (hh + 1) * HEAD_DIM)

    @pl.when(qi == 0)
    def _():
        for hh in heads:
            _build_bias_table(lines_ref.at[hh], bm_scr.at[hh])

    chunk = [_chunk_scores(qi, qc_ref[0, rows(hh), :], kc_ref.at[0, hh], bm_scr.at[hh], p_scr.at[hh])
             for hh in heads]
    for hh in heads:
        acc = _stick_head(qi, q_ref[0, rows(hh), :], k_ref.at[0, hh], v_ref.at[0, hh],
                          z_scr.at[hh], u_scr.at[hh], w_scr.at[hh])
        o_ref[0, rows(hh), :] = acc.astype(BF16)
    for hh in heads:
        kb0, denom = chunk[hh]
        out = jnp.zeros((HEAD_DIM, BLK), F32)
        for j in range(WIN_BLOCKS):
            out = out + jnp.dot(vc_ref[0, hh, kb0 + j], p_scr[hh, j], preferred_element_type=F32)
        oc_ref[0, rows(hh), :] = (out / denom).astype(BF16)


def _attn_call(qa_t, ka, va_t, qc_t, kc, vc_t, rel_bias):
    b, _, s = qa_t.shape
    nb = s // BLK
    lines = _bias_lines(rel_bias)
    hp = ATTN_HEADS
    qspec = pl.BlockSpec((1, hp * HEAD_DIM, BLK), lambda bi, h, qi: (bi, h, qi))
    kspec = pl.BlockSpec((1, hp, nb, BLK, HEAD_DIM), lambda bi, h, qi: (bi, h, 0, 0, 0))
    vspec = pl.BlockSpec((1, hp, nb, HEAD_DIM, BLK), lambda bi, h, qi: (bi, h, 0, 0, 0))
    yshape = jax.ShapeDtypeStruct((b, D_GRP, s), BF16)
    return pl.pallas_call(
        _attn_kernel,
        grid=(b, N_HEADS // hp, nb),
        in_specs=[qspec, kspec, vspec, qspec, kspec, vspec,
                  pl.BlockSpec((hp,) + lines.shape[1:], lambda bi, h, qi: (h, 0, 0))],
        out_specs=[qspec, qspec],
        out_shape=[yshape, yshape],
        scratch_shapes=[pltpu.VMEM((hp, BLK, BLK), F32), pltpu.VMEM((hp, BLK, BLK), F32),
                        pltpu.VMEM((hp, BLK, BLK), BF16), pltpu.VMEM((hp, TABLE_ROWS, BLK), F32),
                        pltpu.VMEM((hp, WIN_BLOCKS, BLK, BLK), BF16)],
        compiler_params=pltpu.CompilerParams(
            dimension_semantics=("parallel", "parallel", "arbitrary"), vmem_limit_bytes=VMEM_LIMIT),
        name="attn",
    )(qa_t, ka, va_t, qc_t, kc, vc_t, lines)


def _bias_lines(rel_bias):
    m = np.arange(2 * BLK)[None, :]
    diff = np.where(m < BLK, m, m - 2 * BLK)
    rel = diff + ((WIN_BLOCKS - 1) - np.arange(TABLE_ROWS // BLK)[:, None]) * BLK
    return rel_bias[:, np.clip(rel, -REL_CLIP, REL_CLIP) + REL_CLIP].astype(F32)


def _layer_norm(v, g, b):
    mu = jnp.mean(v, axis=-1, keepdims=True)
    c = v - mu
    var = jnp.mean(c * c, axis=-1, keepdims=True)
    return c * lax.rsqrt(var + LN_EPS) * g + b


def _post_kernel(alpha, ya_ref, yc_ref, x_ref, woa_ref, woc_ref, g_ref, b_ref, wrh_ref, wrl_ref,
                 br_ref, h_ref, hb_ref, cw_ref):
    mix = _tn_dot(ya_ref[0], woa_ref[...]) + _tn_dot(yc_ref[0], woc_ref[...])
    h = _layer_norm(alpha * x_ref[0] + mix, g_ref[...], b_ref[...])
    h_ref[...] = h
    hb_ref[...] = h.astype(BF16)

    h_hi = h.astype(BF16)
    h_lo = (h - h_hi.astype(F32)).astype(BF16)
    logits = (_nt_dot(wrh_ref[...], h_hi) + _nt_dot(wrh_ref[...], h_lo)
              + _nt_dot(wrl_ref[...], h_hi) + br_ref[...])[0:ROUTER_ROWS]
    row = lax.broadcasted_iota(jnp.int32, logits.shape, 0)
    big = jnp.int32(LANES)
    is_g = (row >= N_EXPERTS) & (row < N_EXPERTS + N_GROUPS)
    gl = jnp.where(is_g, logits, NEG_BIG)
    gmax = jnp.max(gl, axis=0, keepdims=True)
    g_gate = 1.0 / jnp.sum(jnp.exp(gl - gmax), axis=0, keepdims=True)
    g_idx = jnp.min(jnp.where(gl == gmax, row, big), axis=0, keepdims=True) - N_EXPERTS
    in_grp = (row >= g_idx * EXPERTS_PER_GROUP) & (row < (g_idx + 1) * EXPERTS_PER_GROUP)
    el = jnp.where(in_grp, logits, NEG_BIG)
    e1 = jnp.max(el, axis=0, keepdims=True)
    i1 = jnp.min(jnp.where(el == e1, row, big), axis=0, keepdims=True)
    el2 = jnp.where(row == i1, NEG_BIG, el)
    e2 = jnp.max(el2, axis=0, keepdims=True)
    i2 = jnp.min(jnp.where(el2 == e2, row, big), axis=0, keepdims=True)
    r = jnp.exp(e2 - e1)
    gate1 = g_gate / (1.0 + r)
    gate2 = gate1 * r
    cw_t = jnp.where(row == i1, gate1,
                     jnp.where(row == i2, gate2,
                               jnp.where(row == g_idx + N_EXPERTS, 1.0, 0.0)))
    cw_t = jnp.concatenate([cw_t, jnp.zeros((LANES - ROUTER_ROWS, cw_t.shape[1]), F32)], axis=0)
    cw_ref[...] = cw_t.T


def _post_call(ya_t, yc_t, x, woa, woc, g, b, wr, br, alpha):
    bsz, s, d = x.shape
    nb = s // BLK
    n = bsz * s
    wrh = wr.astype(BF16)
    wrl = (wr - wrh.astype(F32)).astype(BF16)
    row = lambda bi, t: (bi * nb + t, 0)
    full = lambda a: pl.BlockSpec(a.shape, lambda bi, t: (0,) * a.ndim)
    return pl.pallas_call(
        functools.partial(_post_kernel, alpha),
        grid=(bsz, nb),
        in_specs=[pl.BlockSpec((1, D_GRP, BLK), lambda bi, t: (bi, 0, t)),
                  pl.BlockSpec((1, D_GRP, BLK), lambda bi, t: (bi, 0, t)),
                  pl.BlockSpec((1, BLK, d), lambda bi, t: (bi, t, 0)),
                  full(woa), full(woc), full(g), full(b), full(wrh), full(wrl), full(br)],
        out_specs=[pl.BlockSpec((BLK, d), row), pl.BlockSpec((BLK, d), row),
                   pl.BlockSpec((BLK, LANES), row)],
        out_shape=[jax.ShapeDtypeStruct((n, d), F32), jax.ShapeDtypeStruct((n, d), BF16),
                   jax.ShapeDtypeStruct((n, LANES), F32)],
        compiler_params=pltpu.CompilerParams(
            dimension_semantics=("parallel", "parallel"), vmem_limit_bytes=VMEM_LIMIT),
        name="post",
    )(ya_t, yc_t, x, woa, woc, g, b, wrh, wrl, br)


MOE_T = 1024
MOE_CH = 128
MOE_VMEM_LIMIT = 60000 * 1024


def _moe_kernel(alpha, hb_ref, h_ref, cw_ref, wg_ref, wu_ref, wd_ref, g_ref, b_ref, o_ref,
                tri_ref, member_ref, rank_ref, rank_t_ref):
    ti = pl.program_id(0)
    g = pl.program_id(1)

    @pl.when((ti == 0) & (g == 0))
    def _():
        r = lax.broadcasted_iota(jnp.int32, (MOE_T, MOE_T), 0)
        c = lax.broadcasted_iota(jnp.int32, (MOE_T, MOE_T), 1)
        tri_ref[...] = (r < c).astype(BF16)

    cwt = cw_ref[...]
    lane = lax.broadcasted_iota(jnp.int32, (MOE_T, LANES), 1)

    @pl.when(g == 0)
    def _():
        o_ref[...] = jnp.zeros_like(o_ref)
        sr = lax.broadcasted_iota(jnp.int32, (LANES, LANES), 0)
        sl = lax.broadcasted_iota(jnp.int32, (LANES, LANES), 1)
        sel = ((sl == sr + N_EXPERTS) & (sr < N_GROUPS)).astype(BF16)
        member = _nt_dot(sel, cwt.astype(BF16))
        rank = jnp.dot(member.astype(BF16), tri_ref[...], preferred_element_type=F32)
        member_ref[...] = member[0:SUBLANES]
        rank_ref[...] = rank[0:SUBLANES]
        rank_t_ref[...] = rank.T

    rowi = lax.broadcasted_iota(jnp.int32, (SUBLANES, MOE_T), 0)
    in_row = jnp.sum(jnp.where(rowi == g, member_ref[...], 0.0), axis=0, keepdims=True)
    rank_row = jnp.sum(jnp.where(rowi == g, rank_ref[...], 0.0), axis=0, keepdims=True)
    rank_col = jnp.sum(jnp.where(lane == g, rank_t_ref[...], 0.0), axis=1, keepdims=True)
    in_col = jnp.sum(jnp.where(lane == g + N_EXPERTS, cwt, 0.0), axis=1, keepdims=True)
    count = jnp.sum(in_row).astype(jnp.int32)
    c1 = cwt.astype(BF16)
    rem = cwt - c1.astype(F32)
    c2 = rem.astype(BF16)
    c3 = (rem - c2.astype(F32)).astype(BF16)
    hb = hb_ref[...]
    df = wg_ref.shape[2] // EXPERTS_PER_GROUP
    chunk_row = lax.broadcasted_iota(jnp.int32, (MOE_CH, MOE_T), 0).astype(F32)
    chunk_lane = lax.broadcasted_iota(jnp.int32, (MOE_T, MOE_CH), 1).astype(F32)
    lane_c = lax.broadcasted_iota(jnp.int32, (MOE_CH, LANES), 1)

    def body(c, carry):
        base = (c * MOE_CH).astype(F32)
        p = ((rank_row == base + chunk_row) & (in_row > 0.0)).astype(BF16)
        x = jnp.dot(p, hb, preferred_element_type=F32).astype(BF16)
        cwc = (jnp.dot(p, c1, preferred_element_type=F32)
               + jnp.dot(p, c2, preferred_element_type=F32)
               + jnp.dot(p, c3, preferred_element_type=F32))
        gate = jnp.dot(x, wg_ref[0], preferred_element_type=F32)
        up = jnp.dot(x, wu_ref[0], preferred_element_type=F32)
        hid = []
        for e in range(EXPERTS_PER_GROUP):
            col = jnp.sum(jnp.where(lane_c == g * EXPERTS_PER_GROUP + e, cwc, 0.0),
                          axis=1, keepdims=True)
            ge = gate[:, e * df:(e + 1) * df]
            hid.append((ge * (1.0 / (1.0 + jnp.exp(-ge))) * up[:, e * df:(e + 1) * df] * col)
                       .astype(BF16))
        out = jnp.dot(jnp.concatenate(hid, axis=1), wd_ref[0], preferred_element_type=F32)
        p_t = ((rank_col == base + chunk_lane) & (in_col > 0.0)).astype(BF16)
        o_ref[...] += jnp.dot(p_t, out.astype(BF16), preferred_element_type=F32)
        return carry

    lax.fori_loop(0, (count + MOE_CH - 1) // MOE_CH, body, 0)

    @pl.when(g == pl.num_programs(1) - 1)
    def _():
        o_ref[...] = _layer_norm(alpha * h_ref[...] + o_ref[...], g_ref[...], b_ref[...])


def _moe_call(hb, h, cw, wg, wu, wd, g, b, alpha):
    n, d = h.shape
    ng, _, dh = wg.shape
    tok = lambda t, q: (t, 0)
    full = lambda a: pl.BlockSpec(a.shape, lambda t, q: (0,) * a.ndim)
    return pl.pallas_call(
        functools.partial(_moe_kernel, alpha),
        grid=(n // MOE_T, ng),
        in_specs=[pl.BlockSpec((MOE_T, d), tok), pl.BlockSpec((MOE_T, d), tok),
                  pl.BlockSpec((MOE_T, LANES), tok),
                  pl.BlockSpec((1, d, dh), lambda t, q: (q, 0, 0)),
                  pl.BlockSpec((1, d, dh), lambda t, q: (q, 0, 0)),
                  pl.BlockSpec((1, dh, d), lambda t, q: (q, 0, 0)),
                  full(g), full(b)],
        out_specs=pl.BlockSpec((MOE_T, d), tok),
        out_shape=jax.ShapeDtypeStruct((n, d), F32),
        scratch_shapes=[pltpu.VMEM((MOE_T, MOE_T), BF16), pltpu.VMEM((SUBLANES, MOE_T), F32),
                        pltpu.VMEM((SUBLANES, MOE_T), F32), pltpu.VMEM((MOE_T, LANES), F32)],
        compiler_params=pltpu.CompilerParams(
            dimension_semantics=("arbitrary", "arbitrary"), vmem_limit_bytes=MOE_VMEM_LIMIT),
        name="moe",
    )(hb, h, cw, wg, wu, wd, g, b)


def _layer(x, w_in, w_out, rel_bias, ln1_g, ln1_b, w_group, b_group, w_expert, b_expert,
           w_gate, w_up, w_down, ln2_g, ln2_b, alpha):
    bsz, s, d = x.shape
    scale = HEAD_DIM ** -0.5
    wqa, wka, wva, wqc, wkc, wvc = [w_in[:, i * D_GRP:(i + 1) * D_GRP] for i in range(6)]
    wnat = jnp.concatenate([wqa * (scale * LOG2E), wqc * scale, wvc], axis=1).T.astype(BF16)
    qa_t, qc_t, vc_t, va_t, ka, kc = _proj_call(
        x, wnat, wva.T.astype(BF16), wka.astype(BF16), wkc.astype(BF16))

    ya_t, yc_t = _attn_call(qa_t, ka, va_t, qc_t, kc, vc_t, rel_bias)

    wo = w_out.astype(BF16)
    wr = jnp.zeros((LANES, d), F32)
    wr = wr.at[:N_EXPERTS].set(w_expert.transpose(0, 2, 1).reshape(N_EXPERTS, d))
    wr = wr.at[N_EXPERTS:N_EXPERTS + N_GROUPS].set(w_group.T)
    br = jnp.zeros((LANES, 1), F32)
    br = br.at[:N_EXPERTS, 0].set(b_expert.reshape(N_EXPERTS))
    br = br.at[N_EXPERTS:N_EXPERTS + N_GROUPS, 0].set(b_group)
    h, hb, cw = _post_call(ya_t, yc_t, x, wo[:D_GRP], wo[D_GRP:], ln1_g[None], ln1_b[None],
                           wr, br, alpha)

    df = w_gate.shape[-1]
    by_group = lambda w: w.astype(BF16).transpose(0, 2, 1, 3).reshape(
        N_GROUPS, d, EXPERTS_PER_GROUP * df)
    out = _moe_call(hb, h, cw, by_group(w_gate), by_group(w_up),
                    w_down.astype(BF16).reshape(N_GROUPS, EXPERTS_PER_GROUP * df, d),
                    ln2_g[None], ln2_b[None], alpha)
    return out.reshape(bsz, s, d)


def kernel(x, w_in, w_out, rel_bias, ln1_g, ln1_b, w_group, b_group, w_expert, b_expert,
           w_gate, w_up, w_down, ln2_g, ln2_b):
    depth = w_in.shape[0]
    alpha = (2.0 * depth) ** 0.25
    for l in range(depth):
        x = _layer(x, w_in[l], w_out[l], rel_bias[l], ln1_g[l], ln1_b[l], w_group[l], b_group[l],
                   w_expert[l], b_expert[l], w_gate[l], w_up[l], w_down[l], ln2_g[l], ln2_b[l], alpha)
    return x
```

```python
import functools

import numpy as np
import jax
import jax.numpy as jnp
from jax import lax
from jax.experimental import pallas as pl
from jax.experimental.pallas import tpu as pltpu

F32 = jnp.float32
BF16 = jnp.bfloat16

HEAD_DIM = 64
N_HEADS = 8
D_GRP = N_HEADS * HEAD_DIM
CHUNK = 64
LEFT_CHUNKS = 8
REL_CLIP = 128
N_GROUPS = 4
EXPERTS_PER_GROUP = 8
N_EXPERTS = N_GROUPS * EXPERTS_PER_GROUP
ROUTER_ROWS = 40
LN_EPS = 1e-5
NEG_BIG = -1e30
LOG2E = float(np.log2(np.e))

SUBLANES = 8
LANES = 128
BLK = 256
ROWGROUPS = BLK // SUBLANES
VMEM_LIMIT = 56 * 1024 * 1024


def _nt_dot(a, b):
    return lax.dot_general(a, b, (((1,), (1,)), ((), ())), preferred_element_type=F32)


def _tn_dot(a, b):
    return lax.dot_general(a, b, (((0,), (0,)), ((), ())), preferred_element_type=F32)


def _proj_kernel(x_ref, perm_ref, wnat_ref, wva_ref, wka_ref, wkc_ref,
                 qa_ref, qc_ref, vc_ref, va_ref, ka_ref, kc_ref):
    xb = x_ref[0].astype(BF16)
    xp = jnp.dot(perm_ref[...], xb, preferred_element_type=F32).astype(BF16)
    nat = _nt_dot(wnat_ref[...], xb)
    qa_ref[0] = nat[0:D_GRP].astype(BF16)
    qc_ref[0] = nat[D_GRP:2 * D_GRP].astype(BF16)
    vc = nat[2 * D_GRP:3 * D_GRP].astype(BF16)
    va = _nt_dot(wva_ref[...], xp).astype(BF16)
    ka = jnp.dot(xp, wka_ref[...], preferred_element_type=F32).astype(BF16)
    kc = jnp.dot(xb, wkc_ref[...], preferred_element_type=F32).astype(BF16)
    for h in range(N_HEADS):
        sl = slice(h * HEAD_DIM, (h + 1) * HEAD_DIM)
        vc_ref[0, h, 0] = vc[sl]
        va_ref[0, h, 0] = va[sl]
        ka_ref[0, h, 0] = ka[:, sl]
        kc_ref[0, h, 0] = kc[:, sl]


def _key_permutation():
    r = np.arange(BLK)
    src = (r % SUBLANES) * ROWGROUPS + (ROWGROUPS - 1 - r // SUBLANES)
    return jnp.asarray(np.eye(BLK, dtype=np.float32)[src], dtype=BF16)


def _proj_call(x, wnat, wva, wka, wkc):
    b, s, d = x.shape
    nb = s // BLK
    perm = _key_permutation()
    qspec = pl.BlockSpec((1, D_GRP, BLK), lambda bi, t: (bi, 0, t))
    vspec = pl.BlockSpec((1, N_HEADS, 1, HEAD_DIM, BLK), lambda bi, t: (bi, 0, t, 0, 0))
    kspec = pl.BlockSpec((1, N_HEADS, 1, BLK, HEAD_DIM), lambda bi, t: (bi, 0, t, 0, 0))
    qshape = jax.ShapeDtypeStruct((b, D_GRP, s), BF16)
    vshape = jax.ShapeDtypeStruct((b, N_HEADS, nb, HEAD_DIM, BLK), BF16)
    kshape = jax.ShapeDtypeStruct((b, N_HEADS, nb, BLK, HEAD_DIM), BF16)
    full = lambda a: pl.BlockSpec(a.shape, lambda bi, t: (0,) * a.ndim)
    return pl.pallas_call(
        _proj_kernel,
        grid=(b, nb),
        in_specs=[pl.BlockSpec((1, BLK, d), lambda bi, t: (bi, t, 0)),
                  full(perm), full(wnat), full(wva), full(wka), full(wkc)],
        out_specs=[qspec, qspec, vspec, vspec, kspec, kspec],
        out_shape=[qshape, qshape, vshape, vshape, kshape, kshape],
        compiler_params=pltpu.CompilerParams(
            dimension_semantics=("parallel", "parallel"), vmem_limit_bytes=VMEM_LIMIT),
        name="proj",
    )(x, perm, wnat, wva, wka, wkc)


def _sublane_shift_up(t, d, sub):
    rolled = pltpu.roll(t, SUBLANES - d, axis=0)
    return jnp.where(sub < SUBLANES - d, rolled, 0.0)


def _stick_scan(z_ref, u_ref, carry, diag_cols, w_ref=None, prev_offset=None):
    n = z_ref.shape[1]
    sub = lax.broadcasted_iota(jnp.int32, (SUBLANES, n), 0)
    run = jnp.zeros((SUBLANES, n), F32)
    prev_w = None
    for i in range(ROWGROUPS):
        rows = slice(i * SUBLANES, (i + 1) * SUBLANES)
        if w_ref is not None:
            w = jnp.exp2(u_ref[rows, :] - prev_offset)
            if i % 2 == 0:
                prev_w = w
            else:
                w_ref[(i - 1) * SUBLANES:(i + 1) * SUBLANES, :] = jnp.concatenate(
                    [prev_w, w], axis=0).astype(BF16)
        zi = z_ref[rows, :]
        sp = jnp.maximum(zi, 0.0) + jnp.log2(1.0 + jnp.exp2(-jnp.abs(zi)))
        if diag_cols is None:
            run = run + sp
            u_ref[rows, :] = zi - run
        else:
            causal = diag_cols > -i
            run = run + jnp.where(causal, sp, 0.0)
            u_ref[rows, :] = jnp.where(causal, zi - run, NEG_BIG)
    s1 = run + _sublane_shift_up(run, 1, sub)
    s2 = s1 + _sublane_shift_up(s1, 2, sub)
    s3 = s2 + _sublane_shift_up(s2, 4, sub)
    offset = (s3 - run) + carry
    new_carry = jnp.broadcast_to(s3[0:1, :], (SUBLANES, n)) + carry
    return offset, new_carry


def _stick_exp(u_ref, offset):
    rows = [jnp.exp2(u_ref[i * SUBLANES:(i + 1) * SUBLANES, :] - offset) for i in range(ROWGROUPS)]
    return jnp.concatenate(rows, axis=0).astype(BF16)


WIN_BLOCKS = LEFT_CHUNKS * CHUNK // BLK + 1
TABLE_ROWS = (2 * WIN_BLOCKS - 1) * BLK


def _build_bias_table(lines_ref, bm_ref):
    row = lax.broadcasted_iota(jnp.int32, (BLK, BLK), 0)
    qchunk = lax.broadcasted_iota(jnp.int32, (BLK, BLK), 1) // CHUNK
    for rb in range(TABLE_ROWS // BLK):
        line = jnp.broadcast_to(lines_ref[rb:rb + 1, :], (BLK, 2 * BLK))
        bias = pltpu.roll(line, 0, 1, stride=1, stride_axis=0)[:, :BLK]
        kchunk = (row + (rb - (WIN_BLOCKS - 1)) * BLK) >> 6
        dchunk = qchunk - kchunk
        visible = (dchunk >= 0) & (dchunk <= LEFT_CHUNKS)
        bm_ref[rb * BLK:(rb + 1) * BLK, :] = jnp.where(visible, bias, NEG_BIG)


def _chunk_scores(qi, q_t, k_ref, bm_ref, p_ref):
    kb0 = jnp.maximum(qi - (WIN_BLOCKS - 1), 0)
    row0 = (WIN_BLOCKS - 1 - (qi - kb0)) * BLK
    scores = []
    for j in range(WIN_BLOCKS):
        start = pl.multiple_of(row0 + j * BLK, BLK)
        sc = jnp.dot(k_ref[kb0 + j], q_t, preferred_element_type=F32)
        scores.append(sc + bm_ref[pl.ds(start, BLK), :])
    m = jnp.max(scores[0], axis=0, keepdims=True)
    for sc in scores[1:]:
        m = jnp.maximum(m, jnp.max(sc, axis=0, keepdims=True))
    denom = jnp.zeros((1, BLK), F32)
    for j in range(WIN_BLOCKS):
        p = jnp.exp(scores[j] - m)
        denom = denom + jnp.sum(p, axis=0, keepdims=True)
        p_ref[j] = p.astype(BF16)
    return kb0, denom


def _stick_head(qi, q_t, k_ref, v_ref, z_scr, u_scr, w_scr):
    last = k_ref.shape[0] - 1
    sub = lax.broadcasted_iota(jnp.int32, (SUBLANES, BLK), 0)
    col = lax.broadcasted_iota(jnp.int32, (SUBLANES, BLK), 1)
    diag_cols = col - ROWGROUPS * sub - (ROWGROUPS - 1)

    def logits(kb):
        return jnp.dot(k_ref[kb], q_t, preferred_element_type=F32)

    z_scr[...] = logits(qi)
    w_scr[...] = jnp.zeros_like(w_scr)
    offset, carry = _stick_scan(z_scr, u_scr, jnp.zeros((SUBLANES, BLK), F32), diag_cols)
    z_scr[...] = logits(jnp.maximum(qi - 1, 0))

    def body(t, state):
        acc, offset, carry = state
        kb = qi - 1 - t
        pv = jnp.dot(v_ref[jnp.minimum(kb + 2, last)], w_scr[...], preferred_element_type=F32)
        offset, carry = _stick_scan(z_scr, u_scr, carry, None, w_scr, offset)
        z_scr[...] = logits(jnp.maximum(kb - 1, 0))
        return acc + pv, offset, carry

    state = (jnp.zeros((HEAD_DIM, BLK), F32), offset, carry)
    odd = qi % 2
    state = lax.cond(odd == 1, lambda s: body(0, s), lambda s: s, state)

    def body_pair(p, s):
        t = odd + 2 * p
        return body(t + 1, body(t, s))

    acc, offset, _ = lax.fori_loop(0, qi // 2, body_pair, state)
    acc = acc + jnp.dot(v_ref[min(1, last)], w_scr[...], preferred_element_type=F32)
    return acc + jnp.dot(v_ref[0], _stick_exp(u_scr, offset), preferred_element_type=F32)


ATTN_HEADS = 4


def _attn_kernel(q_ref, k_ref, v_ref, qc_ref, kc_ref, vc_ref, lines_ref, o_ref, oc_ref,
                 z_scr, u_scr, w_scr, bm_scr, p_scr):
    qi = pl.program_id(2)
    heads = range(ATTN_HEADS)
    rows = lambda hh: slice(hh * HEAD_DIM, (hh + 1) * HEAD_DIM)

    @pl.when(qi == 0)
    def _():
        for hh in heads:
            _build_bias_table(lines_ref.at[hh], bm_scr.at[hh])

    chunk = [_chunk_scores(qi, qc_ref[0, rows(hh), :], kc_ref.at[0, hh], bm_scr.at[hh], p_scr.at[hh])
             for hh in heads]
    for hh in heads:
        acc = _stick_head(qi, q_ref[0, rows(hh), :], k_ref.at[0, hh], v_ref.at[0, hh],
                          z_scr.at[hh], u_scr.at[hh], w_scr.at[hh])
        o_ref[0, rows(hh), :] = acc.astype(BF16)
    for hh in heads:
        kb0, denom = chunk[hh]
        out = jnp.zeros((HEAD_DIM, BLK), F32)
        for j in range(WIN_BLOCKS):
            out = out + jnp.dot(vc_ref[0, hh, kb0 + j], p_scr[hh, j], preferred_element_type=F32)
        oc_ref[0, rows(hh), :] = (out / denom).astype(BF16)


def _attn_call(qa_t, ka, va_t, qc_t, kc, vc_t, rel_bias):
    b, _, s = qa_t.shape
    nb = s // BLK
    lines = _bias_lines(rel_bias)
    hp = ATTN_HEADS
    qspec = pl.BlockSpec((1, hp * HEAD_DIM, BLK), lambda bi, h, qi: (bi, h, qi))
    kspec = pl.BlockSpec((1, hp, nb, BLK, HEAD_DIM), lambda bi, h, qi: (bi, h, 0, 0, 0),
                         pipeline_mode=pl.Buffered(1))
    vspec = pl.BlockSpec((1, hp, nb, HEAD_DIM, BLK), lambda bi, h, qi: (bi, h, 0, 0, 0),
                         pipeline_mode=pl.Buffered(1))
    yshape = jax.ShapeDtypeStruct((b, D_GRP, s), BF16)
    return pl.pallas_call(
        _attn_kernel,
        grid=(b, N_HEADS // hp, nb),
        in_specs=[qspec, kspec, vspec, qspec, kspec, vspec,
                  pl.BlockSpec((hp,) + lines.shape[1:], lambda bi, h, qi: (h, 0, 0))],
        out_specs=[qspec, qspec],
        out_shape=[yshape, yshape],
        scratch_shapes=[pltpu.VMEM((hp, BLK, BLK), F32), pltpu.VMEM((hp, BLK, BLK), F32),
                        pltpu.VMEM((hp, BLK, BLK), BF16), pltpu.VMEM((hp, TABLE_ROWS, BLK), F32),
                        pltpu.VMEM((hp, WIN_BLOCKS, BLK, BLK), BF16)],
        compiler_params=pltpu.CompilerParams(
            dimension_semantics=("parallel", "parallel", "arbitrary"), vmem_limit_bytes=VMEM_LIMIT),
        name="attn",
    )(qa_t, ka, va_t, qc_t, kc, vc_t, lines)


def _bias_lines(rel_bias):
    m = np.arange(2 * BLK)[None, :]
    diff = np.where(m < BLK, m, m - 2 * BLK)
    rel = diff + ((WIN_BLOCKS - 1) - np.arange(TABLE_ROWS // BLK)[:, None]) * BLK
    return rel_bias[:, np.clip(rel, -REL_CLIP, REL_CLIP) + REL_CLIP].astype(F32)


def _layer_norm(v, g, b):
    mu = jnp.mean(v, axis=-1, keepdims=True)
    c = v - mu
    var = jnp.mean(c * c, axis=-1, keepdims=True)
    return c * lax.rsqrt(var + LN_EPS) * g + b


def _post_kernel(alpha, ya_ref, yc_ref, x_ref, woa_ref, woc_ref, g_ref, b_ref, wrh_ref, wrl_ref,
                 br_ref, h_ref, hb_ref, cw_ref):
    mix = _tn_dot(ya_ref[0], woa_ref[...]) + _tn_dot(yc_ref[0], woc_ref[...])
    h = _layer_norm(alpha * x_ref[0] + mix, g_ref[...], b_ref[...])
    h_ref[...] = h
    hb_ref[...] = h.astype(BF16)

    h_hi = h.astype(BF16)
    h_lo = (h - h_hi.astype(F32)).astype(BF16)
    logits = (_nt_dot(wrh_ref[...], h_hi) + _nt_dot(wrh_ref[...], h_lo)
              + _nt_dot(wrl_ref[...], h_hi) + br_ref[...])[0:ROUTER_ROWS]
    row = lax.broadcasted_iota(jnp.int32, logits.shape, 0)
    big = jnp.int32(LANES)
    is_g = (row >= N_EXPERTS) & (row < N_EXPERTS + N_GROUPS)
    gl = jnp.where(is_g, logits, NEG_BIG)
    gmax = jnp.max(gl, axis=0, keepdims=True)
    g_gate = 1.0 / jnp.sum(jnp.exp(gl - gmax), axis=0, keepdims=True)
    g_idx = jnp.min(jnp.where(gl == gmax, row, big), axis=0, keepdims=True) - N_EXPERTS
    in_grp = (row >= g_idx * EXPERTS_PER_GROUP) & (row < (g_idx + 1) * EXPERTS_PER_GROUP)
    el = jnp.where(in_grp, logits, NEG_BIG)
    e1 = jnp.max(el, axis=0, keepdims=True)
    i1 = jnp.min(jnp.where(el == e1, row, big), axis=0, keepdims=True)
    el2 = jnp.where(row == i1, NEG_BIG, el)
    e2 = jnp.max(el2, axis=0, keepdims=True)
    i2 = jnp.min(jnp.where(el2 == e2, row, big), axis=0, keepdims=True)
    r = jnp.exp(e2 - e1)
    gate1 = g_gate / (1.0 + r)
    gate2 = gate1 * r
    cw_t = jnp.where(row == i1, gate1,
                     jnp.where(row == i2, gate2,
                               jnp.where(row == g_idx + N_EXPERTS, 1.0, 0.0)))
    cw_t = jnp.concatenate([cw_t, jnp.zeros((LANES - ROUTER_ROWS, cw_t.shape[1]), F32)], axis=0)
    cw_ref[...] = cw_t.T


def _post_call(ya_t, yc_t, x, woa, woc, g, b, wr, br, alpha):
    bsz, s, d = x.shape
    nb = s // BLK
    n = bsz * s
    wrh = wr.astype(BF16)
    wrl = (wr - wrh.astype(F32)).astype(BF16)
    row = lambda bi, t: (bi * nb + t, 0)
    full = lambda a: pl.BlockSpec(a.shape, lambda bi, t: (0,) * a.ndim)
    return pl.pallas_call(
        functools.partial(_post_kernel, alpha),
        grid=(bsz, nb),
        in_specs=[pl.BlockSpec((1, D_GRP, BLK), lambda bi, t: (bi, 0, t)),
                  pl.BlockSpec((1, D_GRP, BLK), lambda bi, t: (bi, 0, t)),
                  pl.BlockSpec((1, BLK, d), lambda bi, t: (bi, t, 0)),
                  full(woa), full(woc), full(g), full(b), full(wrh), full(wrl), full(br)],
        out_specs=[pl.BlockSpec((BLK, d), row), pl.BlockSpec((BLK, d), row),
                   pl.BlockSpec((BLK, LANES), row)],
        out_shape=[jax.ShapeDtypeStruct((n, d), F32), jax.ShapeDtypeStruct((n, d), BF16),
                   jax.ShapeDtypeStruct((n, LANES), F32)],
        compiler_params=pltpu.CompilerParams(
            dimension_semantics=("parallel", "parallel"), vmem_limit_bytes=VMEM_LIMIT),
        name="post",
    )(ya_t, yc_t, x, woa, woc, g, b, wrh, wrl, br)


MOE_T = 1024
MOE_CH = 128
MOE_VMEM_LIMIT = 60000 * 1024


def _moe_kernel(alpha, hb_ref, h_ref, cw_ref, wg_ref, wu_ref, wd_ref, g_ref, b_ref, o_ref,
                tri_ref, member_ref, rank_ref, rank_t_ref):
    ti = pl.program_id(0)
    g = pl.program_id(1)

    @pl.when((ti == 0) & (g == 0))
    def _():
        r = lax.broadcasted_iota(jnp.int32, (MOE_T, MOE_T), 0)
        c = lax.broadcasted_iota(jnp.int32, (MOE_T, MOE_T), 1)
        tri_ref[...] = (r < c).astype(BF16)

    cwt = cw_ref[...]
    lane = lax.broadcasted_iota(jnp.int32, (MOE_T, LANES), 1)

    @pl.when(g == 0)
    def _():
        o_ref[...] = jnp.zeros_like(o_ref)
        sr = lax.broadcasted_iota(jnp.int32, (LANES, LANES), 0)
        sl = lax.broadcasted_iota(jnp.int32, (LANES, LANES), 1)
        sel = ((sl == sr + N_EXPERTS) & (sr < N_GROUPS)).astype(BF16)
        member = _nt_dot(sel, cwt.astype(BF16))
        rank = jnp.dot(member.astype(BF16), tri_ref[...], preferred_element_type=F32)
        member_ref[...] = member[0:SUBLANES]
        rank_ref[...] = rank[0:SUBLANES]
        rank_t_ref[...] = rank.T

    rowi = lax.broadcasted_iota(jnp.int32, (SUBLANES, MOE_T), 0)
    in_row = jnp.sum(jnp.where(rowi == g, member_ref[...], 0.0), axis=0, keepdims=True)
    rank_row = jnp.sum(jnp.where(rowi == g, rank_ref[...], 0.0), axis=0, keepdims=True)
    rank_col = jnp.sum(jnp.where(lane == g, rank_t_ref[...], 0.0), axis=1, keepdims=True)
    in_col = jnp.sum(jnp.where(lane == g + N_EXPERTS, cwt, 0.0), axis=1, keepdims=True)
    count = jnp.sum(in_row).astype(jnp.int32)
    c1 = cwt.astype(BF16)
    rem = cwt - c1.astype(F32)
    c2 = rem.astype(BF16)
    c3 = (rem - c2.astype(F32)).astype(BF16)
    hb = hb_ref[...]
    df = wg_ref.shape[2] // EXPERTS_PER_GROUP
    chunk_row = lax.broadcasted_iota(jnp.int32, (MOE_CH, MOE_T), 0).astype(F32)
    chunk_lane = lax.broadcasted_iota(jnp.int32, (MOE_T, MOE_CH), 1).astype(F32)
    lane_c = lax.broadcasted_iota(jnp.int32, (MOE_CH, LANES), 1)

    def body(c, carry):
        base = (c * MOE_CH).astype(F32)
        p = ((rank_row == base + chunk_row) & (in_row > 0.0)).astype(BF16)
        x = jnp.dot(p, hb, preferred_element_type=F32).astype(BF16)
        cwc = (jnp.dot(p, c1, preferred_element_type=F32)
               + jnp.dot(p, c2, preferred_element_type=F32)
               + jnp.dot(p, c3, preferred_element_type=F32))
        gate = jnp.dot(x, wg_ref[0], preferred_element_type=F32)
        up = jnp.dot(x, wu_ref[0], preferred_element_type=F32)
        hid = []
        for e in range(EXPERTS_PER_GROUP):
            col = jnp.sum(jnp.where(lane_c == g * EXPERTS_PER_GROUP + e, cwc, 0.0),
                          axis=1, keepdims=True)
            ge = gate[:, e * df:(e + 1) * df]
            hid.append((ge * (1.0 / (1.0 + jnp.exp(-ge))) * up[:, e * df:(e + 1) * df] * col)
                       .astype(BF16))
        out = jnp.dot(jnp.concatenate(hid, axis=1), wd_ref[0], preferred_element_type=F32)
        p_t = ((rank_col == base + chunk_lane) & (in_col > 0.0)).astype(BF16)
        o_ref[...] += jnp.dot(p_t, out.astype(BF16), preferred_element_type=F32)
        return carry

    lax.fori_loop(0, (count + MOE_CH - 1) // MOE_CH, body, 0)

    @pl.when(g == pl.num_programs(1) - 1)
    def _():
        o_ref[...] = _layer_norm(alpha * h_ref[...] + o_ref[...], g_ref[...], b_ref[...])


def _moe_call(hb, h, cw, wg, wu, wd, g, b, alpha):
    n, d = h.shape
    ng, _, dh = wg.shape
    tok = lambda t, q: (t, 0)
    full = lambda a: pl.BlockSpec(a.shape, lambda t, q: (0,) * a.ndim)
    return pl.pallas_call(
        functools.partial(_moe_kernel, alpha),
        grid=(n // MOE_T, ng),
        in_specs=[pl.BlockSpec((MOE_T, d), tok), pl.BlockSpec((MOE_T, d), tok),
                  pl.BlockSpec((MOE_T, LANES), tok),
                  pl.BlockSpec((1, d, dh), lambda t, q: (q, 0, 0)),
                  pl.BlockSpec((1, d, dh), lambda t, q: (q, 0, 0)),
                  pl.BlockSpec((1, dh, d), lambda t, q: (q, 0, 0)),
                  full(g), full(b)],
        out_specs=pl.BlockSpec((MOE_T, d), tok),
        out_shape=jax.ShapeDtypeStruct((n, d), F32),
        scratch_shapes=[pltpu.VMEM((MOE_T, MOE_T), BF16), pltpu.VMEM((SUBLANES, MOE_T), F32),
                        pltpu.VMEM((SUBLANES, MOE_T), F32), pltpu.VMEM((MOE_T, LANES), F32)],
        compiler_params=pltpu.CompilerParams(
            dimension_semantics=("arbitrary", "arbitrary"), vmem_limit_bytes=MOE_VMEM_LIMIT),
        name="moe",
    )(hb, h, cw, wg, wu, wd, g, b)


def _layer(x, w_in, w_out, rel_bias, ln1_g, ln1_b, w_group, b_group, w_expert, b_expert,
           w_gate, w_up, w_down, ln2_g, ln2_b, alpha):
    bsz, s, d = x.shape
    scale = HEAD_DIM ** -0.5
    wqa, wka, wva, wqc, wkc, wvc = [w_in[:, i * D_GRP:(i + 1) * D_GRP] for i in range(6)]
    wnat = jnp.concatenate([wqa * (scale * LOG2E), wqc * scale, wvc], axis=1).T.astype(BF16)
    qa_t, qc_t, vc_t, va_t, ka, kc = _proj_call(
        x, wnat, wva.T.astype(BF16), wka.astype(BF16), wkc.astype(BF16))

    ya_t, yc_t = _attn_call(qa_t, ka, va_t, qc_t, kc, vc_t, rel_bias)

    wo = w_out.astype(BF16)
    wr = jnp.zeros((LANES, d), F32)
    wr = wr.at[:N_EXPERTS].set(w_expert.transpose(0, 2, 1).reshape(N_EXPERTS, d))
    wr = wr.at[N_EXPERTS:N_EXPERTS + N_GROUPS].set(w_group.T)
    br = jnp.zeros((LANES, 1), F32)
    br = br.at[:N_EXPERTS, 0].set(b_expert.reshape(N_EXPERTS))
    br = br.at[N_EXPERTS:N_EXPERTS + N_GROUPS, 0].set(b_group)
    h, hb, cw = _post_call(ya_t, yc_t, x, wo[:D_GRP], wo[D_GRP:], ln1_g[None], ln1_b[None],
                           wr, br, alpha)

    df = w_gate.shape[-1]
    by_group = lambda w: w.astype(BF16).transpose(0, 2, 1, 3).reshape(
        N_GROUPS, d, EXPERTS_PER_GROUP * df)
    out = _moe_call(hb, h, cw, by_group(w_gate), by_group(w_up),
                    w_down.astype(BF16).reshape(N_GROUPS, EXPERTS_PER_GROUP * df, d),
                    ln2_g[None], ln2_b[None], alpha)
    return out.reshape(bsz, s, d)


def kernel(x, w_in, w_out, rel_bias, ln1_g, ln1_b, w_group, b_group, w_expert, b_expert,
           w_gate, w_up, w_down, ln2_g, ln2_b):
    depth = w_in.shape[0]
    alpha = (2.0 * depth) ** 0.25
    for l in range(depth):
        x = _layer(x, w_in[l], w_out[l], rel_bias[l], ln1_g[l], ln1_b[l], w_group[l], b_group[l],
                   w_expert[l], b_expert[l], w_gate[l], w_up[l], w_down[l], ln2_g[l], ln2_b[l], alpha)
    return x
```

```python
import functools

import numpy as np
import jax
import jax.numpy as jnp
from jax import lax
from jax.experimental import pallas as pl
from jax.experimental.pallas import tpu as pltpu

F32 = jnp.float32
BF16 = jnp.bfloat16

HEAD_DIM = 64
N_HEADS = 8
D_GRP = N_HEADS * HEAD_DIM
CHUNK = 64
LEFT_CHUNKS = 8
REL_CLIP = 128
N_GROUPS = 4
EXPERTS_PER_GROUP = 8
N_EXPERTS = N_GROUPS * EXPERTS_PER_GROUP
ROUTER_ROWS = 40
LN_EPS = 1e-5
NEG_BIG = -1e30
LOG2E = float(np.log2(np.e))

SUBLANES = 8
LANES = 128
BLK = 256
ROWGROUPS = BLK // SUBLANES
VMEM_LIMIT = 56 * 1024 * 1024


def _nt_dot(a, b):
    return lax.dot_general(a, b, (((1,), (1,)), ((), ())), preferred_element_type=F32)


def _tn_dot(a, b):
    return lax.dot_general(a, b, (((0,), (0,)), ((), ())), preferred_element_type=F32)


def _proj_kernel(x_ref, perm_ref, wnat_ref, wva_ref, wka_ref, wkc_ref,
                 qa_ref, qc_ref, vc_ref, va_ref, ka_ref, kc_ref):
    xb = x_ref[0].astype(BF16)
    xp = jnp.dot(perm_ref[...], xb, preferred_element_type=F32).astype(BF16)
    nat = _nt_dot(wnat_ref[...], xb)
    qa_ref[0] = nat[0:D_GRP].astype(BF16)
    qc_ref[0] = nat[D_GRP:2 * D_GRP].astype(BF16)
    vc = nat[2 * D_GRP:3 * D_GRP].astype(BF16)
    va = _nt_dot(wva_ref[...], xp).astype(BF16)
    ka = jnp.dot(xp, wka_ref[...], preferred_element_type=F32).astype(BF16)
    kc = jnp.dot(xb, wkc_ref[...], preferred_element_type=F32).astype(BF16)
    for h in range(N_HEADS):
        sl = slice(h * HEAD_DIM, (h + 1) * HEAD_DIM)
        vc_ref[0, h, 0] = vc[sl]
        va_ref[0, h, 0] = va[sl]
        ka_ref[0, h, 0] = ka[:, sl]
        kc_ref[0, h, 0] = kc[:, sl]


def _key_permutation():
    r = np.arange(BLK)
    src = (r % SUBLANES) * ROWGROUPS + (ROWGROUPS - 1 - r // SUBLANES)
    return jnp.asarray(np.eye(BLK, dtype=np.float32)[src], dtype=BF16)


def _proj_call(x, wnat, wva, wka, wkc):
    b, s, d = x.shape
    nb = s // BLK
    perm = _key_permutation()
    qspec = pl.BlockSpec((1, D_GRP, BLK), lambda bi, t: (bi, 0, t))
    vspec = pl.BlockSpec((1, N_HEADS, 1, HEAD_DIM, BLK), lambda bi, t: (bi, 0, t, 0, 0))
    kspec = pl.BlockSpec((1, N_HEADS, 1, BLK, HEAD_DIM), lambda bi, t: (bi, 0, t, 0, 0))
    qshape = jax.ShapeDtypeStruct((b, D_GRP, s), BF16)
    vshape = jax.ShapeDtypeStruct((b, N_HEADS, nb, HEAD_DIM, BLK), BF16)
    kshape = jax.ShapeDtypeStruct((b, N_HEADS, nb, BLK, HEAD_DIM), BF16)
    full = lambda a: pl.BlockSpec(a.shape, lambda bi, t: (0,) * a.ndim)
    return pl.pallas_call(
        _proj_kernel,
        grid=(b, nb),
        in_specs=[pl.BlockSpec((1, BLK, d), lambda bi, t: (bi, t, 0)),
                  full(perm), full(wnat), full(wva), full(wka), full(wkc)],
        out_specs=[qspec, qspec, vspec, vspec, kspec, kspec],
        out_shape=[qshape, qshape, vshape, vshape, kshape, kshape],
        compiler_params=pltpu.CompilerParams(
            dimension_semantics=("parallel", "parallel"), vmem_limit_bytes=VMEM_LIMIT),
        name="proj",
    )(x, perm, wnat, wva, wka, wkc)


def _sublane_shift_up(t, d, sub):
    rolled = pltpu.roll(t, SUBLANES - d, axis=0)
    return jnp.where(sub < SUBLANES - d, rolled, 0.0)


def _stick_scan(z_ref, u_ref, carry, diag_cols, w_ref=None, prev_offset=None):
    n = z_ref.shape[1]
    sub = lax.broadcasted_iota(jnp.int32, (SUBLANES, n), 0)
    run = jnp.zeros((SUBLANES, n), F32)
    prev_w = None
    for i in range(ROWGROUPS):
        rows = slice(i * SUBLANES, (i + 1) * SUBLANES)
        if w_ref is not None:
            w = jnp.exp2(u_ref[rows, :] - prev_offset)
            if i % 2 == 0:
                prev_w = w
            else:
                w_ref[(i - 1) * SUBLANES:(i + 1) * SUBLANES, :] = jnp.concatenate(
                    [prev_w, w], axis=0).astype(BF16)
        zi = z_ref[rows, :]
        sp = jnp.maximum(zi, 0.0) + jnp.log2(1.0 + jnp.exp2(-jnp.abs(zi)))
        if diag_cols is None:
            run = run + sp
            u_ref[rows, :] = zi - run
        else:
            causal = diag_cols > -i
            run = run + jnp.where(causal, sp, 0.0)
            u_ref[rows, :] = jnp.where(causal, zi - run, NEG_BIG)
    s1 = run + _sublane_shift_up(run, 1, sub)
    s2 = s1 + _sublane_shift_up(s1, 2, sub)
    s3 = s2 + _sublane_shift_up(s2, 4, sub)
    offset = (s3 - run) + carry
    new_carry = jnp.broadcast_to(s3[0:1, :], (SUBLANES, n)) + carry
    return offset, new_carry


def _stick_exp(u_ref, offset):
    rows = [jnp.exp2(u_ref[i * SUBLANES:(i + 1) * SUBLANES, :] - offset) for i in range(ROWGROUPS)]
    return jnp.concatenate(rows, axis=0).astype(BF16)


WIN_BLOCKS = LEFT_CHUNKS * CHUNK // BLK + 1
TABLE_ROWS = (2 * WIN_BLOCKS - 1) * BLK


def _build_bias_table(lines_ref, bm_ref):
    row = lax.broadcasted_iota(jnp.int32, (BLK, BLK), 0)
    qchunk = lax.broadcasted_iota(jnp.int32, (BLK, BLK), 1) // CHUNK
    for rb in range(TABLE_ROWS // BLK):
        line = jnp.broadcast_to(lines_ref[rb:rb + 1, :], (BLK, 2 * BLK))
        bias = pltpu.roll(line, 0, 1, stride=1, stride_axis=0)[:, :BLK]
        kchunk = (row + (rb - (WIN_BLOCKS - 1)) * BLK) >> 6
        dchunk = qchunk - kchunk
        visible = (dchunk >= 0) & (dchunk <= LEFT_CHUNKS)
        bm_ref[rb * BLK:(rb + 1) * BLK, :] = jnp.where(visible, bias, NEG_BIG)


def _chunk_scores(qi, q_t, k_ref, bm_ref, p_ref):
    kb0 = jnp.maximum(qi - (WIN_BLOCKS - 1), 0)
    row0 = (WIN_BLOCKS - 1 - (qi - kb0)) * BLK
    scores = []
    for j in range(WIN_BLOCKS):
        start = pl.multiple_of(row0 + j * BLK, BLK)
        sc = jnp.dot(k_ref[kb0 + j], q_t, preferred_element_type=F32)
        scores.append(sc + bm_ref[pl.ds(start, BLK), :])
    m = jnp.max(scores[0], axis=0, keepdims=True)
    for sc in scores[1:]:
        m = jnp.maximum(m, jnp.max(sc, axis=0, keepdims=True))
    denom = jnp.zeros((1, BLK), F32)
    for j in range(WIN_BLOCKS):
        p = jnp.exp(scores[j] - m)
        denom = denom + jnp.sum(p, axis=0, keepdims=True)
        p_ref[j] = p.astype(BF16)
    return kb0, denom


def _stick_head(qi, q_t, k_ref, v_ref, z_scr, u_scr, w_scr):
    last = k_ref.shape[0] - 1
    sub = lax.broadcasted_iota(jnp.int32, (SUBLANES, BLK), 0)
    col = lax.broadcasted_iota(jnp.int32, (SUBLANES, BLK), 1)
    diag_cols = col - ROWGROUPS * sub - (ROWGROUPS - 1)

    def logits(kb):
        return jnp.dot(k_ref[kb], q_t, preferred_element_type=F32)

    z_scr[...] = logits(qi)
    w_scr[...] = jnp.zeros_like(w_scr)
    offset, carry = _stick_scan(z_scr, u_scr, jnp.zeros((SUBLANES, BLK), F32), diag_cols)
    z_scr[...] = logits(jnp.maximum(qi - 1, 0))

    def body(t, state):
        acc, offset, carry = state
        kb = qi - 1 - t
        pv = jnp.dot(v_ref[jnp.minimum(kb + 2, last)], w_scr[...], preferred_element_type=F32)
        offset, carry = _stick_scan(z_scr, u_scr, carry, None, w_scr, offset)
        z_scr[...] = logits(jnp.maximum(kb - 1, 0))
        return acc + pv, offset, carry

    state = (jnp.zeros((HEAD_DIM, BLK), F32), offset, carry)
    odd = qi % 2
    state = lax.cond(odd == 1, lambda s: body(0, s), lambda s: s, state)

    def body_pair(p, s):
        t = odd + 2 * p
        return body(t + 1, body(t, s))

    acc, offset, _ = lax.fori_loop(0, qi // 2, body_pair, state)
    acc = acc + jnp.dot(v_ref[min(1, last)], w_scr[...], preferred_element_type=F32)
    return acc + jnp.dot(v_ref[0], _stick_exp(u_scr, offset), preferred_element_type=F32)


ATTN_HEADS = 4


def _attn_kernel(q_ref, k_ref, v_ref, qc_ref, kc_ref, vc_ref, lines_ref, o_ref, oc_ref,
                 z_scr, u_scr, w_scr, bm_scr, p_scr):
    qi = pl.program_id(2)
    heads = range(ATTN_HEADS)
    rows = lambda hh: slice(hh * HEAD_DIM, (hh + 1) * HEAD_DIM)

    @pl.when(qi == 0)
    def _():
        for hh in heads:
            _build_bias_table(lines_ref.at[hh], bm_scr.at[hh])

    chunk = [_chunk_scores(qi, qc_ref[0, rows(hh), :], kc_ref.at[0, hh], bm_scr.at[hh], p_scr.at[hh])
             for hh in heads]
    for hh in heads:
        acc = _stick_head(qi, q_ref[0, rows(hh), :], k_ref.at[0, hh], v_ref.at[0, hh],
                          z_scr.at[hh], u_scr.at[hh], w_scr.at[hh])
        o_ref[0, rows(hh), :] = acc.astype(BF16)
    for hh in heads:
        kb0, denom = chunk[hh]
        out = jnp.zeros((HEAD_DIM, BLK), F32)
        for j in range(WIN_BLOCKS):
            out = out + jnp.dot(vc_ref[0, hh, kb0 + j], p_scr[hh, j], preferred_element_type=F32)
        oc_ref[0, rows(hh), :] = (out / denom).astype(BF16)


def _attn_call(qa_t, ka, va_t, qc_t, kc, vc_t, rel_bias):
    b, _, s = qa_t.shape
    nb = s // BLK
    lines = _bias_lines(rel_bias)
    hp = ATTN_HEADS
    qspec = pl.BlockSpec((1, hp * HEAD_DIM, BLK), lambda bi, h, qi: (bi, h, qi))
    kspec = pl.BlockSpec((1, hp, nb, BLK, HEAD_DIM), lambda bi, h, qi: (bi, h, 0, 0, 0),
                         pipeline_mode=pl.Buffered(1))
    vspec = pl.BlockSpec((1, hp, nb, HEAD_DIM, BLK), lambda bi, h, qi: (bi, h, 0, 0, 0),
                         pipeline_mode=pl.Buffered(1))
    yshape = jax.ShapeDtypeStruct((b, D_GRP, s), BF16)
    return pl.pallas_call(
        _attn_kernel,
        grid=(b, N_HEADS // hp, nb),
        in_specs=[qspec, kspec, vspec, qspec, kspec, vspec,
                  pl.BlockSpec((hp,) + lines.shape[1:], lambda bi, h, qi: (h, 0, 0))],
        out_specs=[qspec, qspec],
        out_shape=[yshape, yshape],
        scratch_shapes=[pltpu.VMEM((hp, BLK, BLK), F32), pltpu.VMEM((hp, BLK, BLK), F32),
                        pltpu.VMEM((hp, BLK, BLK), BF16), pltpu.VMEM((hp, TABLE_ROWS, BLK), F32),
                        pltpu.VMEM((hp, WIN_BLOCKS, BLK, BLK), BF16)],
        compiler_params=pltpu.CompilerParams(
            dimension_semantics=("parallel", "parallel", "arbitrary"), vmem_limit_bytes=VMEM_LIMIT),
        name="attn",
    )(qa_t, ka, va_t, qc_t, kc, vc_t, lines)


def _bias_lines(rel_bias):
    m = np.arange(2 * BLK)[None, :]
    diff = np.where(m < BLK, m, m - 2 * BLK)
    rel = diff + ((WIN_BLOCKS - 1) - np.arange(TABLE_ROWS // BLK)[:, None]) * BLK
    return rel_bias[:, np.clip(rel, -REL_CLIP, REL_CLIP) + REL_CLIP].astype(F32)


def _layer_norm(v, g, b):
    mu = jnp.mean(v, axis=-1, keepdims=True)
    c = v - mu
    var = jnp.mean(c * c, axis=-1, keepdims=True)
    return c * lax.rsqrt(var + LN_EPS) * g + b


def _post_kernel(alpha, ya_ref, yc_ref, x_ref, woa_ref, woc_ref, g_ref, b_ref, wrh_ref, wrl_ref,
                 br_ref, h_ref, hb_ref, cw_ref):
    mix = _tn_dot(ya_ref[0], woa_ref[...]) + _tn_dot(yc_ref[0], woc_ref[...])
    h = _layer_norm(alpha * x_ref[0] + mix, g_ref[...], b_ref[...])
    h_ref[...] = h
    hb_ref[...] = h.astype(BF16)

    h_hi = h.astype(BF16)
    h_lo = (h - h_hi.astype(F32)).astype(BF16)
    logits = (_nt_dot(wrh_ref[...], h_hi) + _nt_dot(wrh_ref[...], h_lo)
              + _nt_dot(wrl_ref[...], h_hi) + br_ref[...])[0:ROUTER_ROWS]
    row = lax.broadcasted_iota(jnp.int32, logits.shape, 0)
    big = jnp.int32(LANES)
    is_g = (row >= N_EXPERTS) & (row < N_EXPERTS + N_GROUPS)
    gl = jnp.where(is_g, logits, NEG_BIG)
    gmax = jnp.max(gl, axis=0, keepdims=True)
    g_gate = 1.0 / jnp.sum(jnp.exp(gl - gmax), axis=0, keepdims=True)
    g_idx = jnp.min(jnp.where(gl == gmax, row, big), axis=0, keepdims=True) - N_EXPERTS
    in_grp = (row >= g_idx * EXPERTS_PER_GROUP) & (row < (g_idx + 1) * EXPERTS_PER_GROUP)
    el = jnp.where(in_grp, logits, NEG_BIG)
    e1 = jnp.max(el, axis=0, keepdims=True)
    i1 = jnp.min(jnp.where(el == e1, row, big), axis=0, keepdims=True)
    el2 = jnp.where(row == i1, NEG_BIG, el)
    e2 = jnp.max(el2, axis=0, keepdims=True)
    i2 = jnp.min(jnp.where(el2 == e2, row, big), axis=0, keepdims=True)
    r = jnp.exp(e2 - e1)
    gate1 = g_gate / (1.0 + r)
    gate2 = gate1 * r
    cw_t = jnp.where(row == i1, gate1,
                     jnp.where(row == i2, gate2,
                               jnp.where(row == g_idx + N_EXPERTS, 1.0, 0.0)))
    cw_t = jnp.concatenate([cw_t, jnp.zeros((LANES - ROUTER_ROWS, cw_t.shape[1]), F32)], axis=0)
    cw_ref[...] = cw_t.T


def _post_call(ya_t, yc_t, x, woa, woc, g, b, wr, br, alpha):
    bsz, s, d = x.shape
    nb = s // BLK
    n = bsz * s
    wrh = wr.astype(BF16)
    wrl = (wr - wrh.astype(F32)).astype(BF16)
    row = lambda bi, t: (bi * nb + t, 0)
    full = lambda a: pl.BlockSpec(a.shape, lambda bi, t: (0,) * a.ndim)
    return pl.pallas_call(
        functools.partial(_post_kernel, alpha),
        grid=(bsz, nb),
        in_specs=[pl.BlockSpec((1, D_GRP, BLK), lambda bi, t: (bi, 0, t)),
                  pl.BlockSpec((1, D_GRP, BLK), lambda bi, t: (bi, 0, t)),
                  pl.BlockSpec((1, BLK, d), lambda bi, t: (bi, t, 0)),
                  full(woa), full(woc), full(g), full(b), full(wrh), full(wrl), full(br)],
        out_specs=[pl.BlockSpec((BLK, d), row), pl.BlockSpec((BLK, d), row),
                   pl.BlockSpec((BLK, LANES), row)],
        out_shape=[jax.ShapeDtypeStruct((n, d), F32), jax.ShapeDtypeStruct((n, d), BF16),
                   jax.ShapeDtypeStruct((n, LANES), F32)],
        compiler_params=pltpu.CompilerParams(
            dimension_semantics=("parallel", "parallel"), vmem_limit_bytes=VMEM_LIMIT),
        name="post",
    )(ya_t, yc_t, x, woa, woc, g, b, wrh, wrl, br)


MOE_T = 1024
MOE_CH = 128
MOE_VMEM_LIMIT = 60000 * 1024


def _moe_kernel(alpha, hb_ref, h_ref, cw_ref, wg_ref, wu_ref, wd_ref, g_ref, b_ref, o_ref,
                tri_ref, member_ref, rank_ref, rank_t_ref):
    ti = pl.program_id(0)
    g = pl.program_id(1)

    @pl.when((ti == 0) & (g == 0))
    def _():
        r = lax.broadcasted_iota(jnp.int32, (MOE_T, MOE_T), 0)
        c = lax.broadcasted_iota(jnp.int32, (MOE_T, MOE_T), 1)
        tri_ref[...] = (r < c).astype(BF16)

    cwt = cw_ref[...]
    lane = lax.broadcasted_iota(jnp.int32, (MOE_T, LANES), 1)

    @pl.when(g == 0)
    def _():
        o_ref[...] = jnp.zeros_like(o_ref)
        sr = lax.broadcasted_iota(jnp.int32, (LANES, LANES), 0)
        sl = lax.broadcasted_iota(jnp.int32, (LANES, LANES), 1)
        sel = ((sl == sr + N_EXPERTS) & (sr < N_GROUPS)).astype(BF16)
        member = _nt_dot(sel, cwt.astype(BF16))
        rank = jnp.dot(member.astype(BF16), tri_ref[...], preferred_element_type=F32)
        member_ref[...] = member[0:SUBLANES]
        rank_ref[...] = rank[0:SUBLANES]
        rank_t_ref[...] = rank.T

    rowi = lax.broadcasted_iota(jnp.int32, (SUBLANES, MOE_T), 0)
    in_row = jnp.sum(jnp.where(rowi == g, member_ref[...], 0.0), axis=0, keepdims=True)
    rank_row = jnp.sum(jnp.where(rowi == g, rank_ref[...], 0.0), axis=0, keepdims=True)
    rank_col = jnp.sum(jnp.where(lane == g, rank_t_ref[...], 0.0), axis=1, keepdims=True)
    in_col = jnp.sum(jnp.where(lane == g + N_EXPERTS, cwt, 0.0), axis=1, keepdims=True)
    count = jnp.sum(in_row).astype(jnp.int32)
    c1 = cwt.astype(BF16)
    rem = cwt - c1.astype(F32)
    c2 = rem.astype(BF16)
    c3 = (rem - c2.astype(F32)).astype(BF16)
    c123 = jnp.concatenate([c1, c2, c3], axis=1)
    hb = hb_ref[...]
    df = wg_ref.shape[2] // EXPERTS_PER_GROUP
    chunk_row = lax.broadcasted_iota(jnp.int32, (MOE_CH, MOE_T), 0).astype(F32)
    chunk_lane = lax.broadcasted_iota(jnp.int32, (MOE_T, MOE_CH), 1).astype(F32)
    lane_c = lax.broadcasted_iota(jnp.int32, (MOE_CH, LANES), 1)

    def body(c, carry):
        base = (c * MOE_CH).astype(F32)
        p = ((rank_row == base + chunk_row) & (in_row > 0.0)).astype(BF16)
        x = jnp.dot(p, hb, preferred_element_type=F32).astype(BF16)
        parts = jnp.dot(p, c123, preferred_element_type=F32)
        cwc = parts[:, :LANES] + parts[:, LANES:2 * LANES] + parts[:, 2 * LANES:]
        gate = jnp.dot(x, wg_ref[0], preferred_element_type=F32)
        up = jnp.dot(x, wu_ref[0], preferred_element_type=F32)
        hid = []
        for e in range(EXPERTS_PER_GROUP):
            col = jnp.sum(jnp.where(lane_c == g * EXPERTS_PER_GROUP + e, cwc, 0.0),
                          axis=1, keepdims=True)
            ge = gate[:, e * df:(e + 1) * df]
            hid.append((ge * (1.0 / (1.0 + jnp.exp(-ge))) * up[:, e * df:(e + 1) * df] * col)
                       .astype(BF16))
        out = jnp.dot(jnp.concatenate(hid, axis=1), wd_ref[0], preferred_element_type=F32)
        p_t = ((rank_col == base + chunk_lane) & (in_col > 0.0)).astype(BF16)
        o_ref[...] += jnp.dot(p_t, out.astype(BF16), preferred_element_type=F32)
        return carry

    lax.fori_loop(0, (count + MOE_CH - 1) // MOE_CH, body, 0)

    @pl.when(g == pl.num_programs(1) - 1)
    def _():
        o_ref[...] = _layer_norm(alpha * h_ref[...] + o_ref[...], g_ref[...], b_ref[...])


def _moe_call(hb, h, cw, wg, wu, wd, g, b, alpha):
    n, d = h.shape
    ng, _, dh = wg.shape
    tok = lambda t, q: (t, 0)
    full = lambda a: pl.BlockSpec(a.shape, lambda t, q: (0,) * a.ndim)
    return pl.pallas_call(
        functools.partial(_moe_kernel, alpha),
        grid=(n // MOE_T, ng),
        in_specs=[pl.BlockSpec((MOE_T, d), tok), pl.BlockSpec((MOE_T, d), tok),
                  pl.BlockSpec((MOE_T, LANES), tok),
                  pl.BlockSpec((1, d, dh), lambda t, q: (q, 0, 0)),
                  pl.BlockSpec((1, d, dh), lambda t, q: (q, 0, 0)),
                  pl.BlockSpec((1, dh, d), lambda t, q: (q, 0, 0)),
                  full(g), full(b)],
        out_specs=pl.BlockSpec((MOE_T, d), tok),
        out_shape=jax.ShapeDtypeStruct((n, d), F32),
        scratch_shapes=[pltpu.VMEM((MOE_T, MOE_T), BF16), pltpu.VMEM((SUBLANES, MOE_T), F32),
                        pltpu.VMEM((SUBLANES, MOE_T), F32), pltpu.VMEM((MOE_T, LANES), F32)],
        compiler_params=pltpu.CompilerParams(
            dimension_semantics=("arbitrary", "arbitrary"), vmem_limit_bytes=MOE_VMEM_LIMIT),
        name="moe",
    )(hb, h, cw, wg, wu, wd, g, b)


def _layer(x, w_in, w_out, rel_bias, ln1_g, ln1_b, w_group, b_group, w_expert, b_expert,
           w_gate, w_up, w_down, ln2_g, ln2_b, alpha):
    bsz, s, d = x.shape
    scale = HEAD_DIM ** -0.5
    wqa, wka, wva, wqc, wkc, wvc = [w_in[:, i * D_GRP:(i + 1) * D_GRP] for i in range(6)]
    wnat = jnp.concatenate([wqa * (scale * LOG2E), wqc * scale, wvc], axis=1).T.astype(BF16)
    qa_t, qc_t, vc_t, va_t, ka, kc = _proj_call(
        x, wnat, wva.T.astype(BF16), wka.astype(BF16), wkc.astype(BF16))

    ya_t, yc_t = _attn_call(qa_t, ka, va_t, qc_t, kc, vc_t, rel_bias)

    wo = w_out.astype(BF16)
    wr = jnp.zeros((LANES, d), F32)
    wr = wr.at[:N_EXPERTS].set(w_expert.transpose(0, 2, 1).reshape(N_EXPERTS, d))
    wr = wr.at[N_EXPERTS:N_EXPERTS + N_GROUPS].set(w_group.T)
    br = jnp.zeros((LANES, 1), F32)
    br = br.at[:N_EXPERTS, 0].set(b_expert.reshape(N_EXPERTS))
    br = br.at[N_EXPERTS:N_EXPERTS + N_GROUPS, 0].set(b_group)
    h, hb, cw = _post_call(ya_t, yc_t, x, wo[:D_GRP], wo[D_GRP:], ln1_g[None], ln1_b[None],
                           wr, br, alpha)

    df = w_gate.shape[-1]
    by_group = lambda w: w.astype(BF16).transpose(0, 2, 1, 3).reshape(
        N_GROUPS, d, EXPERTS_PER_GROUP * df)
    out = _moe_call(hb, h, cw, by_group(w_gate), by_group(w_up),
                    w_down.astype(BF16).reshape(N_GROUPS, EXPERTS_PER_GROUP * df, d),
                    ln2_g[None], ln2_b[None], alpha)
    return out.reshape(bsz, s, d)


def kernel(x, w_in, w_out, rel_bias, ln1_g, ln1_b, w_group, b_group, w_expert, b_expert,
           w_gate, w_up, w_down, ln2_g, ln2_b):
    depth = w_in.shape[0]
    alpha = (2.0 * depth) ** 0.25
    for l in range(depth):
        x = _layer(x, w_in[l], w_out[l], rel_bias[l], ln1_g[l], ln1_b[l], w_group[l], b_group[l],
                   w_expert[l], b_expert[l], w_gate[l], w_up[l], w_down[l], ln2_g[l], ln2_b[l], alpha)
    return x
```
